```python
import math
import jax, jax.numpy as jnp
from jax import lax
import numpy as np

D_MODEL = 2048
BATCH = 1
SEQ = 16384
DEPTH = 2

N_MIXERS = 2
NORM_EPS = 1e-6
S5_GROUP = 16
S5_GROUPS = D_MODEL // S5_GROUP
S5_STATE = 64
S5_CHUNK = 1024
HEAD_DIM = 64
N_HEADS = D_MODEL // HEAD_DIM
KV_HEADS = N_HEADS // 8
GROUP = N_HEADS // KV_HEADS
Q_DIM = N_HEADS * HEAD_DIM
KV_DIM = KV_HEADS * HEAD_DIM
QKV_DIM = Q_DIM + 2 * KV_DIM
WINDOW = 128
ATTN_BLOCK = 128
D_FF = ((8 * D_MODEL // 3 + 255) // 256) * 256

kernel_name = "hybrid_s5_swa_sink_alibi_swiglu"


def rmsnorm(x, g):
    xf = x.astype(jnp.float32)
    r = xf * lax.rsqrt(jnp.mean(xf * xf, axis=-1, keepdims=True) + NORM_EPS)
    return (r * g.astype(jnp.float32)).astype(x.dtype)


def alibi_slopes():
    h = jnp.arange(1, N_HEADS + 1, dtype=jnp.float32)
    return jnp.exp2(-8.0 * h / N_HEADS)


def _ssm_combine(e1, e2):
    a1r, a1i, b1r, b1i = e1
    a2r, a2i, b2r, b2i = e2
    ar = a2r * a1r - a2i * a1i
    ai = a2r * a1i + a2i * a1r
    br = a2r * b1r - a2i * b1i + b2r
    bi = a2r * b1i + a2i * b1r + b2i
    return ar, ai, br, bi


def s5_mixer(u, a_re, a_im, log_step, b_re, b_im, c_re, c_im, d_skip, w_glu):
    bsz, seq, dm = u.shape
    f32 = jnp.float32
    uf = u.astype(f32)
    dt = jnp.exp(log_step.astype(f32))[:, None]
    lam_re, lam_im = a_re.astype(f32), a_im.astype(f32)
    mag = jnp.exp(lam_re * dt)
    ang = lam_im * dt
    lb_re, lb_im = mag * jnp.cos(ang), mag * jnp.sin(ang)
    n_re, n_im = lb_re - 1.0, lb_im
    den = lam_re * lam_re + lam_im * lam_im
    q_re = (n_re * lam_re + n_im * lam_im) / den
    q_im = (n_im * lam_re - n_re * lam_im) / den
    br_, bi_ = b_re.astype(f32), b_im.astype(f32)
    bb_re = q_re[..., None] * br_ - q_im[..., None] * bi_
    bb_im = q_re[..., None] * bi_ + q_im[..., None] * br_
    cr_, ci_ = c_re.astype(f32), c_im.astype(f32)

    chunk = math.gcd(seq, S5_CHUNK)
    n_chunks = seq // chunk
    u_chunks = uf.reshape(bsz, n_chunks, chunk, S5_GROUPS, S5_GROUP).transpose(1, 0, 2, 3, 4)

    def step(carry, uc):
        h_re, h_im = carry
        bu_re = jnp.einsum('blgc,gpc->blgp', uc, bb_re)
        bu_im = jnp.einsum('blgc,gpc->blgp', uc, bb_im)
        a_re_b = jnp.broadcast_to(lb_re, bu_re.shape)
        a_im_b = jnp.broadcast_to(lb_im, bu_re.shape)
        ar, ai, sr, si = lax.associative_scan(_ssm_combine, (a_re_b, a_im_b, bu_re, bu_im), axis=1)
        s_re = ar * h_re[:, None] - ai * h_im[:, None] + sr
        s_im = ar * h_im[:, None] + ai * h_re[:, None] + si
        y = jnp.einsum('blgp,gcp->blgc', s_re, cr_) - jnp.einsum('blgp,gcp->blgc', s_im, ci_)
        return (s_re[:, -1], s_im[:, -1]), y

    init = (jnp.zeros((bsz, S5_GROUPS, S5_STATE), f32), jnp.zeros((bsz, S5_GROUPS, S5_STATE), f32))
    _, ys = lax.scan(step, init, u_chunks)
    y = ys.transpose(1, 0, 2, 3, 4).reshape(bsz, seq, dm)
    y = y + d_skip.astype(f32) * uf
    g = jax.nn.gelu(y).astype(u.dtype)
    gl = g @ w_glu
    out = gl[..., :dm] * jax.nn.sigmoid(gl[..., dm:])
    return out.astype(u.dtype)


def sliding_window_attention(x, w_qkv, b_qkv, sinks, w_o):
    bsz, seq, _ = x.shape
    nb = seq // ATTN_BLOCK
    qkv = x @ w_qkv + b_qkv
    q = qkv[..., :Q_DIM].reshape(bsz, nb, ATTN_BLOCK, KV_HEADS, GROUP, HEAD_DIM)
    k = qkv[..., Q_DIM:Q_DIM + KV_DIM].reshape(bsz, seq, KV_HEADS, HEAD_DIM)
    v = qkv[..., Q_DIM + KV_DIM:].reshape(bsz, seq, KV_HEADS, HEAD_DIM)

    def band(t):
        tp = jnp.pad(t, ((0, 0), (ATTN_BLOCK, 0), (0, 0), (0, 0)))
        tb = tp.reshape(bsz, nb + 1, ATTN_BLOCK, KV_HEADS, HEAD_DIM)
        return jnp.concatenate([tb[:, :-1], tb[:, 1:]], axis=2)

    kw, vw = band(k), band(v)
    s = jnp.einsum('bnqhgd,bnkhd->bnhgqk', q, kw).astype(jnp.float32) * (HEAD_DIM ** -0.5)
    qi = jnp.arange(ATTN_BLOCK)[:, None]
    kj = jnp.arange(2 * ATTN_BLOCK)[None, :]
    dist = qi + ATTN_BLOCK - kj
    kpos = jnp.arange(nb)[:, None] * ATTN_BLOCK + jnp.arange(2 * ATTN_BLOCK)[None, :] - ATTN_BLOCK
    valid = ((dist >= 0) & (dist < WINDOW))[None] & (kpos >= 0)[:, None, :]
    slopes = alibi_slopes().reshape(KV_HEADS, GROUP)[:, :, None, None]
    s = s - slopes * dist.astype(jnp.float32)
    s = jnp.where(valid[None, :, None, None], s, -jnp.inf)
    sink = sinks.astype(jnp.float32).reshape(KV_HEADS, GROUP)[:, :, None, None]
    m = jnp.maximum(jnp.max(s, axis=-1, keepdims=True), sink)
    p = jnp.exp(s - m)
    p = p / (jnp.sum(p, axis=-1, keepdims=True) + jnp.exp(sink - m))
    o = jnp.einsum('bnhgqk,bnkhd->bnqhgd', p.astype(vw.dtype), vw).reshape(bsz, seq, Q_DIM)
    return (o @ w_o).astype(x.dtype)


def swiglu(x, w_gate, w_up, w_down):
    return (jax.nn.silu(x @ w_gate) * (x @ w_up)) @ w_down


def setup_inputs(seed: int = 0) -> dict:
    key = jax.random.key(seed)
    ks = jax.random.split(key, 20)
    n_a = (DEPTH + 1) // 2
    n_b = DEPTH // 2
    nrm = jax.random.normal
    f32 = jnp.float32
    x = nrm(ks[0], (BATCH, SEQ, D_MODEL), f32)
    norm_mix = 1.0 + 0.02 * nrm(ks[1], (DEPTH, D_MODEL), f32)
    s5_a_re = -0.5 * jnp.exp(0.05 * nrm(ks[2], (n_a, S5_GROUPS, S5_STATE), f32))
    s5_a_im = math.pi * jnp.arange(S5_STATE, dtype=f32)[None, None, :] + 0.02 * nrm(ks[3], (n_a, S5_GROUPS, S5_STATE), f32)
    s5_log_step = jax.random.uniform(ks[4], (n_a, S5_GROUPS), f32, minval=math.log(1e-3), maxval=math.log(1e-1))
    s5_b_re = nrm(ks[5], (n_a, S5_GROUPS, S5_STATE, S5_GROUP), f32) * (2 * S5_GROUP) ** -0.5
    s5_b_im = nrm(ks[6], (n_a, S5_GROUPS, S5_STATE, S5_GROUP), f32) * (2 * S5_GROUP) ** -0.5
    s5_c_re = nrm(ks[7], (n_a, S5_GROUPS, S5_GROUP, S5_STATE), f32) * S5_STATE ** -0.5
    s5_c_im = nrm(ks[8], (n_a, S5_GROUPS, S5_GROUP, S5_STATE), f32) * S5_STATE ** -0.5
    s5_d = nrm(ks[9], (n_a, D_MODEL), f32)
    s5_w_glu = nrm(ks[10], (n_a, D_MODEL, 2 * D_MODEL), f32) * D_MODEL ** -0.5
    attn_w_qkv = nrm(ks[11], (n_b, D_MODEL, QKV_DIM), f32) * D_MODEL ** -0.5
    attn_b_qkv = 0.02 * nrm(ks[12], (n_b, QKV_DIM), f32)
    attn_sinks = 0.5 * nrm(ks[13], (n_b, N_HEADS), f32)
    attn_w_o = nrm(ks[14], (n_b, Q_DIM, D_MODEL), f32) * Q_DIM ** -0.5
    norm_ffn = 1.0 + 0.02 * nrm(ks[15], (DEPTH, D_MODEL), f32)
    ffn_w_gate = nrm(ks[16], (DEPTH, D_MODEL, D_FF), f32) * D_MODEL ** -0.5
    ffn_w_up = nrm(ks[17], (DEPTH, D_MODEL, D_FF), f32) * D_MODEL ** -0.5
    ffn_w_down = nrm(ks[18], (DEPTH, D_FF, D_MODEL), f32) * D_FF ** -0.5
    norm_final = 1.0 + 0.02 * nrm(ks[19], (D_MODEL,), f32)
    return {"x": x, "norm_mix": norm_mix, "s5_a_re": s5_a_re, "s5_a_im": s5_a_im,
            "s5_log_step": s5_log_step, "s5_b_re": s5_b_re, "s5_b_im": s5_b_im,
            "s5_c_re": s5_c_re, "s5_c_im": s5_c_im, "s5_d": s5_d, "s5_w_glu": s5_w_glu,
            "attn_w_qkv": attn_w_qkv, "attn_b_qkv": attn_b_qkv, "attn_sinks": attn_sinks,
            "attn_w_o": attn_w_o, "norm_ffn": norm_ffn, "ffn_w_gate": ffn_w_gate,
            "ffn_w_up": ffn_w_up, "ffn_w_down": ffn_w_down, "norm_final": norm_final}


def reference(x, norm_mix, s5_a_re, s5_a_im, s5_log_step, s5_b_re, s5_b_im, s5_c_re, s5_c_im,
              s5_d, s5_w_glu, attn_w_qkv, attn_b_qkv, attn_sinks, attn_w_o, norm_ffn,
              ffn_w_gate, ffn_w_up, ffn_w_down, norm_final):
    h = x
    for i in range(DEPTH):
        j = i // N_MIXERS
        hn = rmsnorm(h, norm_mix[i])
        if i % N_MIXERS == 0:
            mix = s5_mixer(hn, s5_a_re[j], s5_a_im[j], s5_log_step[j], s5_b_re[j], s5_b_im[j],
                           s5_c_re[j], s5_c_im[j], s5_d[j], s5_w_glu[j])
        else:
            mix = sliding_window_attention(hn, attn_w_qkv[j], attn_b_qkv[j], attn_sinks[j], attn_w_o[j])
        h = h + mix
        hn = rmsnorm(h, norm_ffn[i])
        h = h + swiglu(hn, ffn_w_gate[i], ffn_w_up[i], ffn_w_down[i]).astype(h.dtype)
    return rmsnorm(h, norm_final)
```

```python
import functools
import math

import jax
import jax.numpy as jnp
from jax import lax
from jax.experimental import pallas as pl
from jax.experimental.pallas import tpu as pltpu

F32 = jnp.float32
BF16 = jnp.bfloat16

NORM_EPS = 1e-6
S5_GROUP = 16
S5_STATE = 64
S5_T = 16
HEAD_DIM = 64
Q_PER_KV = 8
ATTN_BLOCK = 128
LANES_V7X = 128
SUBLANES_V7X = 8
VMEM_LIMIT_V7X = 56 * 1024 * 1024


def _cparams(sem):
    return pltpu.CompilerParams(dimension_semantics=sem, vmem_limit_bytes=VMEM_LIMIT_V7X)


def _rms(h, gain):
    return h * lax.rsqrt(jnp.mean(h * h, axis=-1, keepdims=True) + NORM_EPS) * gain


def _norm_kernel(h_ref, g_ref, o_ref):
    o_ref[...] = _rms(h_ref[...], g_ref[...]).astype(o_ref.dtype)


def _norm_cast(h, gain, tm=1024):
    n, d = h.shape
    return pl.pallas_call(
        _norm_kernel,
        grid=(n // tm,),
        in_specs=[pl.BlockSpec((tm, d), lambda i: (i, 0)),
                  pl.BlockSpec((1, d), lambda i: (0, 0))],
        out_specs=pl.BlockSpec((tm, d), lambda i: (i, 0)),
        out_shape=jax.ShapeDtypeStruct((n, d), BF16),
        compiler_params=_cparams(("parallel",)),
        name="norm_cast",
    )(h, gain.reshape(1, d))


def _s5_kernel(x_ref, m_ref, ws_ref, wo_ref, sc_ref, d_ref, o_ref, s_ref, h_ref, *, gb, nchunk):
    for g in range(gb):
        s_ref[g] = jnp.dot(x_ref[g], ws_ref[g], preferred_element_type=F32)

    row = lax.broadcasted_iota(jnp.int32, (SUBLANES_V7X, LANES_V7X), 0)
    zero = jnp.zeros((SUBLANES_V7X, LANES_V7X), F32)

    def body(i, carry):
        r0 = pl.multiple_of(i * SUBLANES_V7X, SUBLANES_V7X)
        out = []
        for g in range(gb):
            cb, cbs = carry[g]
            z = s_ref[g, pl.ds(r0, SUBLANES_V7X), 0:LANES_V7X]
            zs = s_ref[g, pl.ds(r0, SUBLANES_V7X), LANES_V7X:2 * LANES_V7X]
            for qi, q in enumerate((1, 2, 4)):
                a1 = sc_ref[g, 16 * qi:16 * qi + 8, :]
                a2 = sc_ref[g, 16 * qi + 8:16 * qi + 16, :]
                zr = pltpu.roll(z, q, axis=0)
                zsr = pltpu.roll(zs, q, axis=0)
                z, zs = z + a1 * zr + a2 * zsr, zs + a1 * zsr - a2 * zr
            p1 = sc_ref[g, 48:56, :]
            p2 = sc_ref[g, 56:64, :]
            hinc = z + p1 * cb + p2 * cbs
            hincs = zs + p1 * cbs - p2 * cb
            h_ref[g, pl.ds(r0, SUBLANES_V7X), :] = jnp.where(row == 0, cb, pltpu.roll(hinc, 1, axis=0))
            out.append((jnp.broadcast_to(hinc[7:8, :], (SUBLANES_V7X, LANES_V7X)),
                        jnp.broadcast_to(hincs[7:8, :], (SUBLANES_V7X, LANES_V7X))))
        return tuple(out)

    lax.fori_loop(0, nchunk // SUBLANES_V7X, body, tuple((zero, zero) for _ in range(gb)))

    for g in range(gb):
        x = x_ref[g]
        y = jnp.dot(x, m_ref[g], preferred_element_type=F32)
        y = y + jnp.dot(h_ref[g].astype(BF16), wo_ref[g], preferred_element_type=F32)
        y = y + d_ref[g] * x.astype(F32)
        o_ref[g] = jax.nn.gelu(y).astype(o_ref.dtype)


def _s5_core(xg, m, ws, wo, sc, dt, gb=4):
    ng, nchunk, w = xg.shape
    spec3 = lambda a, b: pl.BlockSpec((gb, a, b), lambda i: (i, 0, 0))
    return pl.pallas_call(
        functools.partial(_s5_kernel, gb=gb, nchunk=nchunk),
        grid=(ng // gb,),
        in_specs=[spec3(nchunk, w), spec3(w, w), spec3(w, w), spec3(2 * S5_STATE, w),
                  spec3(64, LANES_V7X), spec3(1, w)],
        out_specs=spec3(nchunk, w),
        out_shape=jax.ShapeDtypeStruct((ng, nchunk, w), BF16),
        scratch_shapes=[pltpu.VMEM((gb, nchunk, w), F32),
                        pltpu.VMEM((gb, nchunk, 2 * S5_STATE), F32)],
        compiler_params=_cparams(("parallel",)),
        name="s5_core",
    )(xg, m, ws, wo, sc, dt)


def _s5_tables(a_re, a_im, log_step, b_re, b_im, c_re, c_im, d_skip):
    ng, ns = a_re.shape
    t = S5_T
    hi = lax.Precision.HIGHEST
    dt = jnp.exp(log_step)[:, None, None]
    lr, li = a_re[:, None, :], a_im[:, None, :]

    def powers(e):
        ee = e.astype(F32)[None, :, None]
        mag = jnp.exp(lr * dt * ee)
        ang = li * dt * ee
        return mag * jnp.cos(ang), mag * jnp.sin(ang)

    lb_re, lb_im = powers(jnp.arange(1, 2))
    n_re, n_im = lb_re - 1.0, lb_im
    den = lr * lr + li * li
    q_re = ((n_re * lr + n_im * li) / den)[:, 0, :, None]
    q_im = ((n_im * lr - n_re * li) / den)[:, 0, :, None]
    bb_re = q_re * b_re - q_im * b_im
    bb_im = q_re * b_im + q_im * b_re

    def c_times(pr, pi):
        cr, ci = c_re[:, None], c_im[:, None]
        pr, pi = pr[:, :, None, :], pi[:, :, None, :]
        return cr * pr - ci * pi, cr * pi + ci * pr

    ca_re, ca_im = c_times(*powers(jnp.arange(0, t)))
    kern = (jnp.einsum("gdcp,gpe->gdec", ca_re, bb_re, precision=hi)
            - jnp.einsum("gdcp,gpe->gdec", ca_im, bb_im, precision=hi))
    lag = jnp.arange(t)[None, :] - jnp.arange(t)[:, None]
    m = jnp.where((lag >= 0)[None, :, :, None, None], kern[:, jnp.clip(lag, 0, t - 1)], 0.0)
    m = m.transpose(0, 1, 3, 2, 4).reshape(ng, t * S5_GROUP, t * S5_GROUP)

    pr, pi = powers(jnp.arange(t - 1, -1, -1))
    pr, pi = pr[:, :, None, :], pi[:, :, None, :]
    br, bi = bb_re.transpose(0, 2, 1)[:, None], bb_im.transpose(0, 2, 1)[:, None]
    ws_re, ws_im = pr * br - pi * bi, pr * bi + pi * br
    ws = jnp.concatenate([ws_re, ws_im, ws_im, ws_re], axis=-1).reshape(ng, t * S5_GROUP, 4 * ns)

    co_re, co_im = c_times(*powers(jnp.arange(1, t + 1)))
    wo = jnp.concatenate([co_re.transpose(0, 3, 1, 2), -co_im.transpose(0, 3, 1, 2)], axis=1)
    wo = wo.reshape(ng, 2 * ns, t * S5_GROUP)

    def pair(e, rows_from):
        er, ei = powers(e)
        keep = (jnp.arange(SUBLANES_V7X) >= rows_from)[None, :, None]
        a1 = jnp.where(keep, jnp.concatenate([er, er], axis=-1), 0.0)
        a2 = jnp.where(keep, jnp.concatenate([-ei, ei], axis=-1), 0.0)
        return [a1, a2]

    sc = []
    for q in (1, 2, 4):
        sc += pair(jnp.full((SUBLANES_V7X,), t * q), q)
    sc += pair(t * jnp.arange(1, SUBLANES_V7X + 1), 0)
    sc = jnp.concatenate(sc, axis=1)

    d_t = jnp.tile(d_skip.reshape(ng, 1, S5_GROUP), (1, 1, t))
    return m.astype(BF16), ws.astype(BF16), wo.astype(BF16), sc, d_t


def _glu_kernel(g_ref, wa_ref, wb_ref, x_ref, o_ref):
    g = g_ref[...]
    a = jnp.dot(g, wa_ref[...], preferred_element_type=F32)
    b = jnp.dot(g, wb_ref[...], preferred_element_type=F32)
    o_ref[...] = x_ref[...] + a / (1.0 + jnp.exp(-b))


def _glu_residual(g, w, x, tm=1024, tn=512):
    n, k = g.shape
    d = x.shape[1]
    nj = d // tn
    return pl.pallas_call(
        _glu_kernel,
        grid=(n // tm, nj),
        in_specs=[pl.BlockSpec((tm, k), lambda i, j: (i, 0)),
                  pl.BlockSpec((k, tn), lambda i, j: (0, j)),
                  pl.BlockSpec((k, tn), lambda i, j: (0, j + nj)),
                  pl.BlockSpec((tm, tn), lambda i, j: (i, j))],
        out_specs=pl.BlockSpec((tm, tn), lambda i, j: (i, j)),
        out_shape=jax.ShapeDtypeStruct((n, d), F32),
        compiler_params=_cparams(("parallel", "arbitrary")),
        name="glu_residual",
    )(g, w, w, x)


def _linres_kernel(a_ref, w_ref, r_ref, o_ref):
    o_ref[...] = r_ref[...] + jnp.dot(a_ref[...], w_ref[...], preferred_element_type=F32)


def _linear_residual(a, w, r, tm=1024, tn=512):
    n, k = a.shape
    d = w.shape[1]
    return pl.pallas_call(
        _linres_kernel,
        grid=(n // tm, d // tn),
        in_specs=[pl.BlockSpec((tm, k), lambda i, j: (i, 0)),
                  pl.BlockSpec((k, tn), lambda i, j: (0, j)),
                  pl.BlockSpec((tm, tn), lambda i, j: (i, j))],
        out_specs=pl.BlockSpec((tm, tn), lambda i, j: (i, j)),
        out_shape=jax.ShapeDtypeStruct((n, d), F32),
        compiler_params=_cparams(("parallel", "arbitrary")),
        name="linear_residual",
    )(a, w, r)


def _ffn_kernel(h_ref, gn_ref, wg_ref, wu_ref, wd_ref, gf_ref, o_ref, hn_ref, *, final_norm):
    j = pl.program_id(1)

    @pl.when(j == 0)
    def _():
        h = h_ref[...]
        hn_ref[...] = _rms(h, gn_ref[...]).astype(hn_ref.dtype)
        o_ref[...] = h

    hn = hn_ref[...]
    gate = jnp.dot(hn, wg_ref[...], preferred_element_type=F32)
    up = jnp.dot(hn, wu_ref[...], preferred_element_type=F32)
    act = (gate / (1.0 + jnp.exp(-gate)) * up).astype(BF16)
    o_ref[...] += jnp.dot(act, wd_ref[...], preferred_element_type=F32)

    if final_norm:
        @pl.when(j == pl.num_programs(1) - 1)
        def _():
            o_ref[...] = _rms(o_ref[...], gf_ref[...])


def _ffn(h, gain, wg, wu, wd, gain_final, final_norm, tm=512, tf=512):
    n, d = h.shape
    f = wg.shape[1]
    return pl.pallas_call(
        functools.partial(_ffn_kernel, final_norm=final_norm),
        grid=(n // tm, f // tf),
        in_specs=[pl.BlockSpec((tm, d), lambda i, j: (i, 0)),
                  pl.BlockSpec((1, d), lambda i, j: (0, 0)),
                  pl.BlockSpec((d, tf), lambda i, j: (0, j)),
                  pl.BlockSpec((d, tf), lambda i, j: (0, j)),
                  pl.BlockSpec((tf, d), lambda i, j: (j, 0)),
                  pl.BlockSpec((1, d), lambda i, j: (0, 0))],
        out_specs=pl.BlockSpec((tm, d), lambda i, j: (i, 0)),
        out_shape=jax.ShapeDtypeStruct((n, d), F32),
        scratch_shapes=[pltpu.VMEM((tm, d), BF16)],
        compiler_params=_cparams(("parallel", "arbitrary")),
        name="ffn_final" if final_norm else "ffn",
    )(h, gain.reshape(1, d), wg, wu, wd, gain_final.reshape(1, d))


def _qkv_kernel(h_ref, gn_ref, w_ref, b_ref, o_ref, hn_ref):
    @pl.when(pl.program_id(1) == 0)
    def _():
        hn_ref[...] = _rms(h_ref[...], gn_ref[...]).astype(hn_ref.dtype)

    acc = jnp.dot(hn_ref[...], w_ref[...], preferred_element_type=F32)
    o_ref[...] = (acc + b_ref[...]).astype(o_ref.dtype)


def _qkv(h, gain, w, b, tm=1024, tn=512):
    n, d = h.shape
    m = w.shape[1]
    return pl.pallas_call(
        _qkv_kernel,
        grid=(n // tm, m // tn),
        in_specs=[pl.BlockSpec((tm, d), lambda i, j: (i, 0)),
                  pl.BlockSpec((1, d), lambda i, j: (0, 0)),
                  pl.BlockSpec((d, tn), lambda i, j: (0, j)),
                  pl.BlockSpec((1, tn), lambda i, j: (0, j))],
        out_specs=pl.BlockSpec((tm, tn), lambda i, j: (i, j)),
        out_shape=jax.ShapeDtypeStruct((n, m), BF16),
        scratch_shapes=[pltpu.VMEM((tm, d), BF16)],
        compiler_params=_cparams(("parallel", "arbitrary")),
        name="qkv",
    )(h, gain.reshape(1, d), w, b.reshape(1, m))


def _attn_kernel(slope_ref, sink_ref, q_ref, kv_ref, kvp_ref, o_ref, kbuf_ref, *, tq, kv_heads):
    blk = ATTN_BLOCK
    kvw = kv_heads * HEAD_DIM
    step = pl.program_id(0)
    kbuf_ref[0:blk, :] = kvp_ref[...]
    kbuf_ref[blk:, :] = kv_ref[...]

    qi = lax.broadcasted_iota(jnp.int32, (blk, 2 * blk), 0)
    kj = lax.broadcasted_iota(jnp.int32, (blk, 2 * blk), 1)
    dist = qi + blk - kj
    neg_dist = jnp.where((dist >= 0) & (dist < blk), -dist.astype(F32), -jnp.inf)

    def body(b, _):
        r0 = pl.multiple_of(b * blk, blk)
        first = jnp.logical_and(step == 0, b == 0)
        nd = jnp.where(kj < jnp.where(first, blk, 0), -jnp.inf, neg_dist)
        for kvh in range(kv_heads):
            k = kbuf_ref[pl.ds(r0, 2 * blk), kvh * HEAD_DIM:(kvh + 1) * HEAD_DIM]
            v = kbuf_ref[pl.ds(r0, 2 * blk), kvw + kvh * HEAD_DIM:kvw + (kvh + 1) * HEAD_DIM]
            heads = [kvh * Q_PER_KV + g for g in range(Q_PER_KV)]
            q8 = jnp.concatenate(
                [q_ref[pl.ds(r0, blk), h * HEAD_DIM:(h + 1) * HEAD_DIM] for h in heads], axis=0)
            s8 = lax.dot_general(q8, k, (((1,), (1,)), ((), ())), preferred_element_type=F32)
            ps, dens = [], []
            for g, h in enumerate(heads):
                s = s8[g * blk:(g + 1) * blk] * (HEAD_DIM ** -0.5) + slope_ref[h] * nd
                sink = sink_ref[h]
                mx = jnp.maximum(jnp.max(s, axis=-1, keepdims=True), sink)
                p = jnp.exp(s - mx)
                dens.append(jnp.sum(p, axis=-1, keepdims=True) + jnp.exp(sink - mx))
                ps.append(p.astype(BF16))
            o8 = jnp.dot(jnp.concatenate(ps, axis=0), v, preferred_element_type=F32)
            outs = [o8[g * blk:(g + 1) * blk] / dens[g] for g in range(Q_PER_KV)]
            lo = kvh * Q_PER_KV * HEAD_DIM
            o_ref[pl.ds(r0, blk), lo:lo + Q_PER_KV * HEAD_DIM] = (
                jnp.concatenate(outs, axis=-1).astype(o_ref.dtype))
        return 0

    lax.fori_loop(0, tq // blk, body, 0)


def _attention(qkv, slopes, sinks, q_dim, kv_heads, tq=512):
    n = qkv.shape[0]
    kvw = 2 * kv_heads * HEAD_DIM
    kv_col = q_dim // kvw
    per = tq // ATTN_BLOCK
    return pl.pallas_call(
        functools.partial(_attn_kernel, tq=tq, kv_heads=kv_heads),
        grid=(n // tq,),
        in_specs=[pl.BlockSpec(memory_space=pltpu.SMEM),
                  pl.BlockSpec(memory_space=pltpu.SMEM),
                  pl.BlockSpec((tq, q_dim), lambda i: (i, 0)),
                  pl.BlockSpec((tq, kvw), lambda i: (i, kv_col)),
                  pl.BlockSpec((ATTN_BLOCK, kvw), lambda i: (jnp.maximum(i * per - 1, 0), kv_col))],
        out_specs=pl.BlockSpec((tq, q_dim), lambda i: (i, 0)),
        out_shape=jax.ShapeDtypeStruct((n, q_dim), BF16),
        scratch_shapes=[pltpu.VMEM((tq + ATTN_BLOCK, kvw), BF16)],
        compiler_params=_cparams(("parallel",)),
        name="swa_attention",
    )(slopes, sinks, qkv, qkv, qkv)


def kernel(x, norm_mix, s5_a_re, s5_a_im, s5_log_step, s5_b_re, s5_b_im, s5_c_re, s5_c_im, s5_d, s5_w_glu, attn_w_qkv, attn_b_qkv, attn_sinks, attn_w_o, norm_ffn, ffn_w_gate, ffn_w_up, ffn_w_down, norm_final):
    bsz, seq, dm = x.shape
    ng = dm // S5_GROUP
    nchunk = seq // S5_T
    n_heads = attn_sinks.shape[1]
    q_dim = n_heads * HEAD_DIM
    kv_heads = n_heads // Q_PER_KV
    assert bsz == 1 and norm_mix.shape[0] == 2 and seq % 1024 == 0
    assert S5_T * S5_GROUP == 2 * LANES_V7X and s5_a_re.shape[2] == S5_STATE
    assert q_dim % (2 * kv_heads * HEAD_DIM) == 0

    h0 = x.reshape(seq, dm)

    hn = _norm_cast(h0, norm_mix[0])
    xg = hn.reshape(nchunk, S5_T, ng, S5_GROUP).transpose(2, 0, 1, 3).reshape(ng, nchunk, S5_T * S5_GROUP)
    tables = _s5_tables(s5_a_re[0], s5_a_im[0], s5_log_step[0], s5_b_re[0], s5_b_im[0],
                        s5_c_re[0], s5_c_im[0], s5_d[0])
    yg = _s5_core(xg, *tables)
    gact = yg.reshape(ng, nchunk, S5_T, S5_GROUP).transpose(1, 2, 0, 3).reshape(seq, dm)
    h1 = _glu_residual(gact, s5_w_glu[0].astype(BF16), h0)
    h2 = _ffn(h1, norm_ffn[0], ffn_w_gate[0].astype(BF16), ffn_w_up[0].astype(BF16),
              ffn_w_down[0].astype(BF16), norm_final, final_norm=False)

    qkv = _qkv(h2, norm_mix[1], attn_w_qkv[0].astype(BF16), attn_b_qkv[0])
    slopes = jnp.exp2(-8.0 * jnp.arange(1, n_heads + 1, dtype=F32) / n_heads)
    o = _attention(qkv, slopes, attn_sinks[0], q_dim, kv_heads)
    h3 = _linear_residual(o, attn_w_o[0].astype(BF16), h2)
    out = _ffn(h3, norm_ffn[1], ffn_w_gate[1].astype(BF16), ffn_w_up[1].astype(BF16),
               ffn_w_down[1].astype(BF16), norm_final, final_norm=True)
    return out.reshape(bsz, seq, dm)
```

```python
import functools
import math

import jax
import jax.numpy as jnp
from jax import lax
from jax.experimental import pallas as pl
from jax.experimental.pallas import tpu as pltpu

F32 = jnp.float32
BF16 = jnp.bfloat16

NORM_EPS = 1e-6
S5_GROUP = 16
S5_STATE = 64
S5_T = 16
HEAD_DIM = 64
Q_PER_KV = 8
ATTN_BLOCK = 128
LANES_V7X = 128
SUBLANES_V7X = 8
VMEM_LIMIT_V7X = 56 * 1024 * 1024


def _cparams(sem):
    return pltpu.CompilerParams(dimension_semantics=sem, vmem_limit_bytes=VMEM_LIMIT_V7X)


def _rms(h, gain):
    return h * lax.rsqrt(jnp.mean(h * h, axis=-1, keepdims=True) + NORM_EPS) * gain


GROUPS_PER_TILE = LANES_V7X // S5_GROUP
REGROUP_CHUNKS = 64
REGROUP_PITCH = REGROUP_CHUNKS + 8


def _slot_bits():
    shape = (SUBLANES_V7X, LANES_V7X)
    sub = lax.broadcasted_iota(jnp.int32, shape, 0)
    lane = lax.broadcasted_iota(jnp.int32, shape, 1)
    return ([((sub >> b) & 1) == 1 for b in range(3)],
            [((lane >> (4 + b)) & 1) == 1 for b in range(3)])


def _time_to_group_vreg(x, bits):
    sub_bits, slot_bits = bits
    for b in range(3):
        x = jnp.where(sub_bits[b], pltpu.roll(x, LANES_V7X - (S5_GROUP << b), 1), x)
    for b in range(3):
        x = jnp.where(slot_bits[b], pltpu.roll(x, 1 << b, 0), x)
    return x


def _group_to_time_vreg(z, bits):
    sub_bits, slot_bits = bits
    for b in range(3):
        z = jnp.where(slot_bits[b], pltpu.roll(z, SUBLANES_V7X - (1 << b), 0), z)
    for b in range(3):
        z = jnp.where(sub_bits[b], pltpu.roll(z, S5_GROUP << b, 1), z)
    return z


def _norm_regroup_kernel(h_ref, g_ref, o_ref, inv_ref, z_ref):
    jb, pitch = REGROUP_CHUNKS, REGROUP_PITCH
    ntile = h_ref.shape[1] // LANES_V7X
    h = h_ref[...]
    inv = lax.rsqrt(jnp.mean(h * h, axis=-1, keepdims=True) + NORM_EPS)
    inv_ref[...] = jnp.broadcast_to(inv, inv_ref.shape)
    bits = _slot_bits()

    def body(j, _):
        for half in range(2):
            r0 = pl.multiple_of(j * S5_T + half * SUBLANES_V7X, SUBLANES_V7X)
            scale = inv_ref[pl.ds(r0, SUBLANES_V7X), :]
            for t in range(ntile):
                lanes = slice(t * LANES_V7X, (t + 1) * LANES_V7X)
                x = h_ref[pl.ds(r0, SUBLANES_V7X), lanes] * scale * g_ref[:, lanes]
                z_ref[2 * t + half, pl.ds(j, SUBLANES_V7X, stride=pitch), :] = _time_to_group_vreg(x, bits)
        return 0

    lax.fori_loop(0, jb, body, 0)
    for g in range(o_ref.shape[0]):
        t, g8 = divmod(g, GROUPS_PER_TILE)
        rows = slice(g8 * pitch, g8 * pitch + jb)
        o_ref[g] = jnp.concatenate([z_ref[2 * t, rows, :], z_ref[2 * t + 1, rows, :]],
                                   axis=-1).astype(o_ref.dtype)


def _norm_regroup(h, gain):
    n, d = h.shape
    tm = REGROUP_CHUNKS * S5_T
    ng = d // S5_GROUP
    return pl.pallas_call(
        _norm_regroup_kernel,
        grid=(n // tm,),
        in_specs=[pl.BlockSpec((tm, d), lambda i: (i, 0)),
                  pl.BlockSpec((1, d), lambda i: (0, 0))],
        out_specs=pl.BlockSpec((ng, REGROUP_CHUNKS, S5_T * S5_GROUP), lambda i: (0, i, 0)),
        out_shape=jax.ShapeDtypeStruct((ng, n // S5_T, S5_T * S5_GROUP), BF16),
        scratch_shapes=[pltpu.VMEM((tm, LANES_V7X), F32),
                        pltpu.VMEM((2 * d // LANES_V7X, GROUPS_PER_TILE * REGROUP_PITCH, LANES_V7X), F32)],
        compiler_params=_cparams(("parallel",)),
        name="norm_regroup",
    )(h, gain.reshape(1, d))


def _s5_kernel(x_ref, m_ref, ws_ref, wo_ref, sc_ref, d_ref, o_ref, s_ref, h_ref, *, gb, nchunk):
    for g in range(gb):
        s_ref[g] = jnp.dot(x_ref[g], ws_ref[g], preferred_element_type=F32)

    row = lax.broadcasted_iota(jnp.int32, (SUBLANES_V7X, LANES_V7X), 0)
    zero = jnp.zeros((SUBLANES_V7X, LANES_V7X), F32)

    def body(i, carry):
        r0 = pl.multiple_of(i * SUBLANES_V7X, SUBLANES_V7X)
        out = []
        for g in range(gb):
            cb, cbs = carry[g]
            z = s_ref[g, pl.ds(r0, SUBLANES_V7X), 0:LANES_V7X]
            zs = s_ref[g, pl.ds(r0, SUBLANES_V7X), LANES_V7X:2 * LANES_V7X]
            for qi, q in enumerate((1, 2, 4)):
                a1 = sc_ref[g, 16 * qi:16 * qi + 8, :]
                a2 = sc_ref[g, 16 * qi + 8:16 * qi + 16, :]
                zr = pltpu.roll(z, q, axis=0)
                zsr = pltpu.roll(zs, q, axis=0)
                z, zs = z + a1 * zr + a2 * zsr, zs + a1 * zsr - a2 * zr
            p1 = sc_ref[g, 48:56, :]
            p2 = sc_ref[g, 56:64, :]
            hinc = z + p1 * cb + p2 * cbs
            hincs = zs + p1 * cbs - p2 * cb
            h_ref[g, pl.ds(r0, SUBLANES_V7X), :] = jnp.where(row == 0, cb, pltpu.roll(hinc, 1, axis=0))
            out.append((jnp.broadcast_to(hinc[7:8, :], (SUBLANES_V7X, LANES_V7X)),
                        jnp.broadcast_to(hincs[7:8, :], (SUBLANES_V7X, LANES_V7X))))
        return tuple(out)

    lax.fori_loop(0, nchunk // SUBLANES_V7X, body, tuple((zero, zero) for _ in range(gb)))

    for g in range(gb):
        x = x_ref[g]
        y = jnp.dot(x, m_ref[g], preferred_element_type=F32)
        y = y + jnp.dot(h_ref[g].astype(BF16), wo_ref[g], preferred_element_type=F32)
        y = y + d_ref[g] * x.astype(F32)
        o_ref[g] = jax.nn.gelu(y).astype(o_ref.dtype)


def _s5_core(xg, m, ws, wo, sc, dt, gb=4):
    ng, nchunk, w = xg.shape
    spec3 = lambda a, b: pl.BlockSpec((gb, a, b), lambda i: (i, 0, 0))
    return pl.pallas_call(
        functools.partial(_s5_kernel, gb=gb, nchunk=nchunk),
        grid=(ng // gb,),
        in_specs=[spec3(nchunk, w), spec3(w, w), spec3(w, w), spec3(2 * S5_STATE, w),
                  spec3(64, LANES_V7X), spec3(1, w)],
        out_specs=spec3(nchunk, w),
        out_shape=jax.ShapeDtypeStruct((ng, nchunk, w), BF16),
        scratch_shapes=[pltpu.VMEM((gb, nchunk, w), F32),
                        pltpu.VMEM((gb, nchunk, 2 * S5_STATE), F32)],
        compiler_params=_cparams(("parallel",)),
        name="s5_core",
    )(xg, m, ws, wo, sc, dt)


def _s5_tables(a_re, a_im, log_step, b_re, b_im, c_re, c_im, d_skip):
    ng, ns = a_re.shape
    t = S5_T
    hi = lax.Precision.HIGHEST
    dt = jnp.exp(log_step)[:, None, None]
    lr, li = a_re[:, None, :], a_im[:, None, :]

    def powers(e):
        ee = e.astype(F32)[None, :, None]
        mag = jnp.exp(lr * dt * ee)
        ang = li * dt * ee
        return mag * jnp.cos(ang), mag * jnp.sin(ang)

    lb_re, lb_im = powers(jnp.arange(1, 2))
    n_re, n_im = lb_re - 1.0, lb_im
    den = lr * lr + li * li
    q_re = ((n_re * lr + n_im * li) / den)[:, 0, :, None]
    q_im = ((n_im * lr - n_re * li) / den)[:, 0, :, None]
    bb_re = q_re * b_re - q_im * b_im
    bb_im = q_re * b_im + q_im * b_re

    def c_times(pr, pi):
        cr, ci = c_re[:, None], c_im[:, None]
        pr, pi = pr[:, :, None, :], pi[:, :, None, :]
        return cr * pr - ci * pi, cr * pi + ci * pr

    ca_re, ca_im = c_times(*powers(jnp.arange(0, t)))
    kern = (jnp.einsum("gdcp,gpe->gdec", ca_re, bb_re, precision=hi)
            - jnp.einsum("gdcp,gpe->gdec", ca_im, bb_im, precision=hi))
    lag = jnp.arange(t)[None, :] - jnp.arange(t)[:, None]
    m = jnp.where((lag >= 0)[None, :, :, None, None], kern[:, jnp.clip(lag, 0, t - 1)], 0.0)
    m = m.transpose(0, 1, 3, 2, 4).reshape(ng, t * S5_GROUP, t * S5_GROUP)

    pr, pi = powers(jnp.arange(t - 1, -1, -1))
    pr, pi = pr[:, :, None, :], pi[:, :, None, :]
    br, bi = bb_re.transpose(0, 2, 1)[:, None], bb_im.transpose(0, 2, 1)[:, None]
    ws_re, ws_im = pr * br - pi * bi, pr * bi + pi * br
    ws = jnp.concatenate([ws_re, ws_im, ws_im, ws_re], axis=-1).reshape(ng, t * S5_GROUP, 4 * ns)

    co_re, co_im = c_times(*powers(jnp.arange(1, t + 1)))
    wo = jnp.concatenate([co_re.transpose(0, 3, 1, 2), -co_im.transpose(0, 3, 1, 2)], axis=1)
    wo = wo.reshape(ng, 2 * ns, t * S5_GROUP)

    def pair(e, rows_from):
        er, ei = powers(e)
        keep = (jnp.arange(SUBLANES_V7X) >= rows_from)[None, :, None]
        a1 = jnp.where(keep, jnp.concatenate([er, er], axis=-1), 0.0)
        a2 = jnp.where(keep, jnp.concatenate([-ei, ei], axis=-1), 0.0)
        return [a1, a2]

    sc = []
    for q in (1, 2, 4):
        sc += pair(jnp.full((SUBLANES_V7X,), t * q), q)
    sc += pair(t * jnp.arange(1, SUBLANES_V7X + 1), 0)
    sc = jnp.concatenate(sc, axis=1)

    d_t = jnp.tile(d_skip.reshape(ng, 1, S5_GROUP), (1, 1, t))

    r = jnp.arange(t * S5_GROUP)
    half, slot, chan = r // LANES_V7X, (r % LANES_V7X) // S5_GROUP, r % S5_GROUP
    g8 = jnp.arange(ng)[:, None] % GROUPS_PER_TILE
    step = SUBLANES_V7X * half[None] + (g8 - slot[None]) % SUBLANES_V7X
    idx = step * S5_GROUP + chan[None]
    m = jnp.take_along_axis(jnp.take_along_axis(m, idx[:, :, None], 1), idx[:, None, :], 2)
    ws = jnp.take_along_axis(ws, idx[:, :, None], 1)
    wo = jnp.take_along_axis(wo, idx[:, None, :], 2)
    return m.astype(BF16), ws.astype(BF16), wo.astype(BF16), sc, d_t


def _glu_kernel(yg_ref, wa_ref, wb_ref, x_ref, o_ref, z_ref, lhs_ref):
    jb, pitch = REGROUP_CHUNKS, REGROUP_PITCH
    ntile = lhs_ref.shape[1] // LANES_V7X

    @pl.when(pl.program_id(1) == 0)
    def _():
        for g in range(yg_ref.shape[0]):
            t, g8 = divmod(g, GROUPS_PER_TILE)
            y = yg_ref[g].astype(F32)
            rows = slice(g8 * pitch, g8 * pitch + jb)
            z_ref[2 * t, rows, :] = y[:, :LANES_V7X]
            z_ref[2 * t + 1, rows, :] = y[:, LANES_V7X:]
        bits = _slot_bits()

        def body(j, _):
            r0 = pl.multiple_of(j * S5_T, S5_T)
            for t in range(ntile):
                halves = [_group_to_time_vreg(z_ref[2 * t + half, pl.ds(j, SUBLANES_V7X, stride=pitch), :], bits)
                          for half in range(2)]
                lhs_ref[pl.ds(r0, S5_T), t * LANES_V7X:(t + 1) * LANES_V7X] = (
                    jnp.concatenate(halves, axis=0).astype(lhs_ref.dtype))
            return 0

        lax.fori_loop(0, jb, body, 0)

    g = lhs_ref[...]
    a = jnp.dot(g, wa_ref[...], preferred_element_type=F32)
    b = jnp.dot(g, wb_ref[...], preferred_element_type=F32)
    o_ref[...] = x_ref[...] + a / (1.0 + jnp.exp(-b))


def _glu_residual(yg, w, layer, x, tn=512):
    ng, nchunk, width = yg.shape
    n, d = x.shape
    k = ng * S5_GROUP
    tm = REGROUP_CHUNKS * S5_T
    nj = d // tn
    return pl.pallas_call(
        _glu_kernel,
        grid=(n // tm, nj),
        in_specs=[pl.BlockSpec((ng, REGROUP_CHUNKS, width), lambda i, j: (0, i, 0)),
                  pl.BlockSpec((None, k, tn), lambda i, j: (layer, 0, j)),
                  pl.BlockSpec((None, k, tn), lambda i, j: (layer, 0, j + nj)),
                  pl.BlockSpec((tm, tn), lambda i, j: (i, j))],
        out_specs=pl.BlockSpec((tm, tn), lambda i, j: (i, j)),
        out_shape=jax.ShapeDtypeStruct((n, d), F32),
        scratch_shapes=[pltpu.VMEM((2 * k // LANES_V7X, GROUPS_PER_TILE * REGROUP_PITCH, LANES_V7X), F32),
                        pltpu.VMEM((tm, k), BF16)],
        compiler_params=_cparams(("parallel", "arbitrary")),
        name="glu_residual",
    )(yg, w, w, x)


def _linres_kernel(a_ref, w_ref, r_ref, o_ref):
    o_ref[...] = r_ref[...] + jnp.dot(a_ref[...], w_ref[...], preferred_element_type=F32)


def _linear_residual(a, w, layer, r, tm=1024, tn=512):
    n, k = a.shape
    d = w.shape[2]
    return pl.pallas_call(
        _linres_kernel,
        grid=(n // tm, d // tn),
        in_specs=[pl.BlockSpec((tm, k), lambda i, j: (i, 0)),
                  pl.BlockSpec((None, k, tn), lambda i, j: (layer, 0, j)),
                  pl.BlockSpec((tm, tn), lambda i, j: (i, j))],
        out_specs=pl.BlockSpec((tm, tn), lambda i, j: (i, j)),
        out_shape=jax.ShapeDtypeStruct((n, d), F32),
        compiler_params=_cparams(("parallel", "arbitrary")),
        name="linear_residual",
    )(a, w, r)


def _ffn_kernel(h_ref, gn_ref, wg_ref, wu_ref, wd_ref, gf_ref, o_ref, hn_ref, *, final_norm):
    j = pl.program_id(1)

    @pl.when(j == 0)
    def _():
        h = h_ref[...]
        hn_ref[...] = _rms(h, gn_ref[...]).astype(hn_ref.dtype)
        o_ref[...] = h

    hn = hn_ref[...]
    gate = jnp.dot(hn, wg_ref[...], preferred_element_type=F32)
    up = jnp.dot(hn, wu_ref[...], preferred_element_type=F32)
    act = (gate / (1.0 + jnp.exp(-gate)) * up).astype(BF16)
    o_ref[...] += jnp.dot(act, wd_ref[...], preferred_element_type=F32)

    if final_norm:
        @pl.when(j == pl.num_programs(1) - 1)
        def _():
            o_ref[...] = _rms(o_ref[...], gf_ref[...])


def _ffn(h, gain, wg, wu, wd, layer, gain_final, final_norm, tm=512, tf=512):
    n, d = h.shape
    f = wg.shape[2]
    return pl.pallas_call(
        functools.partial(_ffn_kernel, final_norm=final_norm),
        grid=(n // tm, f // tf),
        in_specs=[pl.BlockSpec((tm, d), lambda i, j: (i, 0)),
                  pl.BlockSpec((1, d), lambda i, j: (0, 0)),
                  pl.BlockSpec((None, d, tf), lambda i, j: (layer, 0, j)),
                  pl.BlockSpec((None, d, tf), lambda i, j: (layer, 0, j)),
                  pl.BlockSpec((None, tf, d), lambda i, j: (layer, j, 0)),
                  pl.BlockSpec((1, d), lambda i, j: (0, 0))],
        out_specs=pl.BlockSpec((tm, d), lambda i, j: (i, 0)),
        out_shape=jax.ShapeDtypeStruct((n, d), F32),
        scratch_shapes=[pltpu.VMEM((tm, d), BF16)],
        compiler_params=_cparams(("parallel", "arbitrary")),
        name="ffn_final" if final_norm else "ffn",
    )(h, gain.reshape(1, d), wg, wu, wd, gain_final.reshape(1, d))


def _qkv_kernel(h_ref, gn_ref, w_ref, b_ref, o_ref, hn_ref):
    @pl.when(pl.program_id(1) == 0)
    def _():
        hn_ref[...] = _rms(h_ref[...], gn_ref[...]).astype(hn_ref.dtype)

    acc = jnp.dot(hn_ref[...], w_ref[...], preferred_element_type=F32)
    o_ref[...] = (acc + b_ref[...]).astype(o_ref.dtype)


def _qkv(h, gain, w, layer, b, tm=1024, tn=512):
    n, d = h.shape
    m = w.shape[2]
    return pl.pallas_call(
        _qkv_kernel,
        grid=(n // tm, m // tn),
        in_specs=[pl.BlockSpec((tm, d), lambda i, j: (i, 0)),
                  pl.BlockSpec((1, d), lambda i, j: (0, 0)),
                  pl.BlockSpec((None, d, tn), lambda i, j: (layer, 0, j)),
                  pl.BlockSpec((1, tn), lambda i, j: (0, j))],
        out_specs=pl.BlockSpec((tm, tn), lambda i, j: (i, j)),
        out_shape=jax.ShapeDtypeStruct((n, m), BF16),
        scratch_shapes=[pltpu.VMEM((tm, d), BF16)],
        compiler_params=_cparams(("parallel", "arbitrary")),
        name="qkv",
    )(h, gain.reshape(1, d), w, b.reshape(1, m))


def _attn_kernel(slope_ref, sink_ref, q_ref, kv_ref, kvp_ref, o_ref, kbuf_ref, *, tq, kv_heads):
    blk = ATTN_BLOCK
    kvw = kv_heads * HEAD_DIM
    step = pl.program_id(0)
    kbuf_ref[0:blk, :] = kvp_ref[...]
    kbuf_ref[blk:, :] = kv_ref[...]

    qi = lax.broadcasted_iota(jnp.int32, (blk, 2 * blk), 0)
    kj = lax.broadcasted_iota(jnp.int32, (blk, 2 * blk), 1)
    dist = qi + blk - kj
    neg_dist = jnp.where((dist >= 0) & (dist < blk), -dist.astype(F32), -jnp.inf)

    def body(b, _):
        r0 = pl.multiple_of(b * blk, blk)
        first = jnp.logical_and(step == 0, b == 0)
        nd = jnp.where(kj < jnp.where(first, blk, 0), -jnp.inf, neg_dist)
        for kvh in range(kv_heads):
            k = kbuf_ref[pl.ds(r0, 2 * blk), kvh * HEAD_DIM:(kvh + 1) * HEAD_DIM]
            v = kbuf_ref[pl.ds(r0, 2 * blk), kvw + kvh * HEAD_DIM:kvw + (kvh + 1) * HEAD_DIM]
            heads = [kvh * Q_PER_KV + g for g in range(Q_PER_KV)]
            q8 = jnp.concatenate(
                [q_ref[pl.ds(r0, blk), h * HEAD_DIM:(h + 1) * HEAD_DIM] for h in heads], axis=0)
            s8 = lax.dot_general(q8, k, (((1,), (1,)), ((), ())), preferred_element_type=F32)
            ps, dens = [], []
            for g, h in enumerate(heads):
                s = s8[g * blk:(g + 1) * blk] * (HEAD_DIM ** -0.5) + slope_ref[h] * nd
                sink = sink_ref[h]
                mx = jnp.maximum(jnp.max(s, axis=-1, keepdims=True), sink)
                p = jnp.exp(s - mx)
                dens.append(jnp.sum(p, axis=-1, keepdims=True) + jnp.exp(sink - mx))
                ps.append(p.astype(BF16))
            o8 = jnp.dot(jnp.concatenate(ps, axis=0), v, preferred_element_type=F32)
            outs = [o8[g * blk:(g + 1) * blk] / dens[g] for g in range(Q_PER_KV)]
            lo = kvh * Q_PER_KV * HEAD_DIM
            o_ref[pl.ds(r0, blk), lo:lo + Q_PER_KV * HEAD_DIM] = (
                jnp.concatenate(outs, axis=-1).astype(o_ref.dtype))
        return 0

    lax.fori_loop(0, tq // blk, body, 0)


def _attention(qkv, slopes, sinks, q_dim, kv_heads, tq=512):
    n = qkv.shape[0]
    kvw = 2 * kv_heads * HEAD_DIM
    kv_col = q_dim // kvw
    per = tq // ATTN_BLOCK
    return pl.pallas_call(
        functools.partial(_attn_kernel, tq=tq, kv_heads=kv_heads),
        grid=(n // tq,),
        in_specs=[pl.BlockSpec(memory_space=pltpu.SMEM),
                  pl.BlockSpec(memory_space=pltpu.SMEM),
                  pl.BlockSpec((tq, q_dim), lambda i: (i, 0)),
                  pl.BlockSpec((tq, kvw), lambda i: (i, kv_col)),
                  pl.BlockSpec((ATTN_BLOCK, kvw), lambda i: (jnp.maximum(i * per - 1, 0), kv_col))],
        out_specs=pl.BlockSpec((tq, q_dim), lambda i: (i, 0)),
        out_shape=jax.ShapeDtypeStruct((n, q_dim), BF16),
        scratch_shapes=[pltpu.VMEM((tq + ATTN_BLOCK, kvw), BF16)],
        compiler_params=_cparams(("parallel",)),
        name="swa_attention",
    )(slopes, sinks, qkv, qkv, qkv)


def kernel(x, norm_mix, s5_a_re, s5_a_im, s5_log_step, s5_b_re, s5_b_im, s5_c_re, s5_c_im, s5_d, s5_w_glu, attn_w_qkv, attn_b_qkv, attn_sinks, attn_w_o, norm_ffn, ffn_w_gate, ffn_w_up, ffn_w_down, norm_final):
    bsz, seq, dm = x.shape
    ng = dm // S5_GROUP
    nchunk = seq // S5_T
    n_heads = attn_sinks.shape[1]
    q_dim = n_heads * HEAD_DIM
    kv_heads = n_heads // Q_PER_KV
    assert bsz == 1 and norm_mix.shape[0] == 2 and seq % 1024 == 0
    assert S5_T * S5_GROUP == 2 * LANES_V7X and s5_a_re.shape[2] == S5_STATE
    assert q_dim % (2 * kv_heads * HEAD_DIM) == 0

    h0 = x.reshape(seq, dm)

    w_gate, w_up, w_down = (w.astype(BF16) for w in (ffn_w_gate, ffn_w_up, ffn_w_down))
    xg = _norm_regroup(h0, norm_mix[0])
    tables = _s5_tables(s5_a_re[0], s5_a_im[0], s5_log_step[0], s5_b_re[0], s5_b_im[0],
                        s5_c_re[0], s5_c_im[0], s5_d[0])
    yg = _s5_core(xg, *tables)
    h1 = _glu_residual(yg, s5_w_glu.astype(BF16), 0, h0)
    h2 = _ffn(h1, norm_ffn[0], w_gate, w_up, w_down, 0, norm_final, final_norm=False)

    qkv = _qkv(h2, norm_mix[1], attn_w_qkv.astype(BF16), 0, attn_b_qkv[0])
    slopes = jnp.exp2(-8.0 * jnp.arange(1, n_heads + 1, dtype=F32) / n_heads)
    o = _attention(qkv, slopes, attn_sinks[0], q_dim, kv_heads)
    h3 = _linear_residual(o, attn_w_o.astype(BF16), 0, h2)
    out = _ffn(h3, norm_ffn[1], w_gate, w_up, w_down, 1, norm_final, final_norm=True)
    return out.reshape(bsz, seq, dm)
```

```python
import functools
import math

import jax
import jax.numpy as jnp
from jax import lax
from jax.experimental import pallas as pl
from jax.experimental.pallas import tpu as pltpu

F32 = jnp.float32
BF16 = jnp.bfloat16

NORM_EPS = 1e-6
S5_GROUP = 16
S5_STATE = 64
S5_T = 16
HEAD_DIM = 64
Q_PER_KV = 8
ATTN_BLOCK = 128
LANES_V7X = 128
SUBLANES_V7X = 8
VMEM_LIMIT_V7X = 56 * 1024 * 1024


def _cparams(sem):
    return pltpu.CompilerParams(dimension_semantics=sem, vmem_limit_bytes=VMEM_LIMIT_V7X)


def _rms(h, gain):
    return h * lax.rsqrt(jnp.mean(h * h, axis=-1, keepdims=True) + NORM_EPS) * gain


GROUPS_PER_TILE = LANES_V7X // S5_GROUP
REGROUP_CHUNKS = 64
REGROUP_PITCH = REGROUP_CHUNKS + 8


def _slot_bits():
    shape = (SUBLANES_V7X, LANES_V7X)
    sub = lax.broadcasted_iota(jnp.int32, shape, 0)
    lane = lax.broadcasted_iota(jnp.int32, shape, 1)
    return ([((sub >> b) & 1) == 1 for b in range(3)],
            [((lane >> (4 + b)) & 1) == 1 for b in range(3)])


def _time_to_group_vreg(x, bits):
    sub_bits, slot_bits = bits
    for b in range(3):
        x = jnp.where(sub_bits[b], pltpu.roll(x, LANES_V7X - (S5_GROUP << b), 1), x)
    for b in range(3):
        x = jnp.where(slot_bits[b], pltpu.roll(x, 1 << b, 0), x)
    return x


def _group_to_time_vreg(z, bits):
    sub_bits, slot_bits = bits
    for b in range(3):
        z = jnp.where(slot_bits[b], pltpu.roll(z, SUBLANES_V7X - (1 << b), 0), z)
    for b in range(3):
        z = jnp.where(sub_bits[b], pltpu.roll(z, S5_GROUP << b, 1), z)
    return z


def _norm_regroup_kernel(h_ref, g_ref, o_ref, inv_ref, z_ref):
    jb, pitch = REGROUP_CHUNKS, REGROUP_PITCH
    ntile = h_ref.shape[1] // LANES_V7X
    h = h_ref[...]
    inv = lax.rsqrt(jnp.mean(h * h, axis=-1, keepdims=True) + NORM_EPS)
    inv_ref[...] = jnp.broadcast_to(inv, inv_ref.shape)
    bits = _slot_bits()

    def body(j, _):
        for half in range(2):
            r0 = pl.multiple_of(j * S5_T + half * SUBLANES_V7X, SUBLANES_V7X)
            scale = inv_ref[pl.ds(r0, SUBLANES_V7X), :]
            for t in range(ntile):
                lanes = slice(t * LANES_V7X, (t + 1) * LANES_V7X)
                x = h_ref[pl.ds(r0, SUBLANES_V7X), lanes] * scale * g_ref[:, lanes]
                z_ref[2 * t + half, pl.ds(j, SUBLANES_V7X, stride=pitch), :] = _time_to_group_vreg(x, bits)
        return 0

    lax.fori_loop(0, jb, body, 0)
    for g in range(o_ref.shape[0]):
        t, g8 = divmod(g, GROUPS_PER_TILE)
        rows = slice(g8 * pitch, g8 * pitch + jb)
        o_ref[g] = jnp.concatenate([z_ref[2 * t, rows, :], z_ref[2 * t + 1, rows, :]],
                                   axis=-1).astype(o_ref.dtype)


def _norm_regroup(h, gain):
    n, d = h.shape
    tm = REGROUP_CHUNKS * S5_T
    ng = d // S5_GROUP
    return pl.pallas_call(
        _norm_regroup_kernel,
        grid=(n // tm,),
        in_specs=[pl.BlockSpec((tm, d), lambda i: (i, 0)),
                  pl.BlockSpec((1, d), lambda i: (0, 0))],
        out_specs=pl.BlockSpec((ng, REGROUP_CHUNKS, S5_T * S5_GROUP), lambda i: (0, i, 0)),
        out_shape=jax.ShapeDtypeStruct((ng, n // S5_T, S5_T * S5_GROUP), BF16),
        scratch_shapes=[pltpu.VMEM((tm, LANES_V7X), F32),
                        pltpu.VMEM((2 * d // LANES_V7X, GROUPS_PER_TILE * REGROUP_PITCH, LANES_V7X), F32)],
        compiler_params=_cparams(("parallel",)),
        name="norm_regroup",
    )(h, gain.reshape(1, d))


def _s5_kernel(x_ref, m_ref, ws_ref, wo_ref, sc_ref, d_ref, o_ref, s_ref, h_ref, *, gb, nchunk):
    for g in range(gb):
        s_ref[g] = jnp.dot(x_ref[g], ws_ref[g], preferred_element_type=F32)

    row = lax.broadcasted_iota(jnp.int32, (SUBLANES_V7X, LANES_V7X), 0)
    zero = jnp.zeros((SUBLANES_V7X, LANES_V7X), F32)

    def body(i, carry):
        r0 = pl.multiple_of(i * SUBLANES_V7X, SUBLANES_V7X)
        out = []
        for g in range(gb):
            cb, cbs = carry[g]
            z = s_ref[g, pl.ds(r0, SUBLANES_V7X), 0:LANES_V7X]
            zs = s_ref[g, pl.ds(r0, SUBLANES_V7X), LANES_V7X:2 * LANES_V7X]
            for qi, q in enumerate((1, 2, 4)):
                a1 = sc_ref[g, 16 * qi:16 * qi + 8, :]
                a2 = sc_ref[g, 16 * qi + 8:16 * qi + 16, :]
                zr = pltpu.roll(z, q, axis=0)
                zsr = pltpu.roll(zs, q, axis=0)
                z, zs = z + a1 * zr + a2 * zsr, zs + a1 * zsr - a2 * zr
            p1 = sc_ref[g, 48:56, :]
            p2 = sc_ref[g, 56:64, :]
            hinc = z + p1 * cb + p2 * cbs
            hincs = zs + p1 * cbs - p2 * cb
            h_ref[g, pl.ds(r0, SUBLANES_V7X), :] = jnp.where(row == 0, cb, pltpu.roll(hinc, 1, axis=0))
            out.append((jnp.broadcast_to(hinc[7:8, :], (SUBLANES_V7X, LANES_V7X)),
                        jnp.broadcast_to(hincs[7:8, :], (SUBLANES_V7X, LANES_V7X))))
        return tuple(out)

    lax.fori_loop(0, nchunk // SUBLANES_V7X, body, tuple((zero, zero) for _ in range(gb)))

    for g in range(gb):
        x = x_ref[g]
        y = jnp.dot(x, m_ref[g], preferred_element_type=F32)
        y = y + jnp.dot(h_ref[g].astype(BF16), wo_ref[g], preferred_element_type=F32)
        y = y + d_ref[g] * x.astype(F32)
        o_ref[g] = jax.nn.gelu(y).astype(o_ref.dtype)


def _s5_core(xg, m, ws, wo, sc, dt, gb=4):
    ng, nchunk, w = xg.shape
    spec3 = lambda a, b: pl.BlockSpec((gb, a, b), lambda i: (i, 0, 0))
    return pl.pallas_call(
        functools.partial(_s5_kernel, gb=gb, nchunk=nchunk),
        grid=(ng // gb,),
        in_specs=[spec3(nchunk, w), spec3(w, w), spec3(w, w), spec3(2 * S5_STATE, w),
                  spec3(64, LANES_V7X), spec3(1, w)],
        out_specs=spec3(nchunk, w),
        out_shape=jax.ShapeDtypeStruct((ng, nchunk, w), BF16),
        scratch_shapes=[pltpu.VMEM((gb, nchunk, w), F32),
                        pltpu.VMEM((gb, nchunk, 2 * S5_STATE), F32)],
        compiler_params=_cparams(("parallel",)),
        name="s5_core",
    )(xg, m, ws, wo, sc, dt)


def _s5_tables(a_re, a_im, log_step, b_re, b_im, c_re, c_im, d_skip):
    ng, ns = a_re.shape
    t = S5_T
    dt = jnp.exp(log_step)[:, None, None]
    lr, li = a_re[:, None, :], a_im[:, None, :]

    def powers(e):
        ee = e.astype(F32)[None, :, None]
        mag = jnp.exp(lr * dt * ee)
        ang = li * dt * ee
        return mag * jnp.cos(ang), mag * jnp.sin(ang)

    lb_re, lb_im = powers(jnp.arange(1, 2))
    n_re, n_im = lb_re - 1.0, lb_im
    den = lr * lr + li * li
    q_re = ((n_re * lr + n_im * li) / den)[:, 0, :, None]
    q_im = ((n_im * lr - n_re * li) / den)[:, 0, :, None]
    bb_re = q_re * b_re - q_im * b_im
    bb_im = q_re * b_im + q_im * b_re

    def c_times(pr, pi):
        cr, ci = c_re[:, None], c_im[:, None]
        pr, pi = pr[:, :, None, :], pi[:, :, None, :]
        return cr * pr - ci * pi, cr * pi + ci * pr

    ca_re, ca_im = c_times(*powers(jnp.arange(0, t)))
    bt_re, bt_im = bb_re.transpose(0, 2, 1)[:, None, :, None], bb_im.transpose(0, 2, 1)[:, None, :, None]
    kern = jnp.sum(ca_re[:, :, None] * bt_re - ca_im[:, :, None] * bt_im, axis=-1)
    lag = jnp.arange(t)[None, :] - jnp.arange(t)[:, None]
    m = jnp.where((lag >= 0)[None, :, :, None, None], kern[:, jnp.clip(lag, 0, t - 1)], 0.0)
    m = m.transpose(0, 1, 3, 2, 4).reshape(ng, t * S5_GROUP, t * S5_GROUP)

    pr, pi = powers(jnp.arange(t - 1, -1, -1))
    pr, pi = pr[:, :, None, :], pi[:, :, None, :]
    br, bi = bb_re.transpose(0, 2, 1)[:, None], bb_im.transpose(0, 2, 1)[:, None]
    ws_re, ws_im = pr * br - pi * bi, pr * bi + pi * br
    ws = jnp.concatenate([ws_re, ws_im, ws_im, ws_re], axis=-1).reshape(ng, t * S5_GROUP, 4 * ns)

    co_re, co_im = c_times(*powers(jnp.arange(1, t + 1)))
    wo = jnp.concatenate([co_re.transpose(0, 3, 1, 2), -co_im.transpose(0, 3, 1, 2)], axis=1)
    wo = wo.reshape(ng, 2 * ns, t * S5_GROUP)

    def pair(e, rows_from):
        er, ei = powers(e)
        keep = (jnp.arange(SUBLANES_V7X) >= rows_from)[None, :, None]
        a1 = jnp.where(keep, jnp.concatenate([er, er], axis=-1), 0.0)
        a2 = jnp.where(keep, jnp.concatenate([-ei, ei], axis=-1), 0.0)
        return [a1, a2]

    sc = []
    for q in (1, 2, 4):
        sc += pair(jnp.full((SUBLANES_V7X,), t * q), q)
    sc += pair(t * jnp.arange(1, SUBLANES_V7X + 1), 0)
    sc = jnp.concatenate(sc, axis=1)

    d_t = jnp.tile(d_skip.reshape(ng, 1, S5_GROUP), (1, 1, t))

    r = jnp.arange(t * S5_GROUP)
    half, slot, chan = r // LANES_V7X, (r % LANES_V7X) // S5_GROUP, r % S5_GROUP
    g8 = jnp.arange(ng)[:, None] % GROUPS_PER_TILE
    step = SUBLANES_V7X * half[None] + (g8 - slot[None]) % SUBLANES_V7X
    idx = step * S5_GROUP + chan[None]
    perm = (idx[:, None, :] == r[None, :, None]).astype(BF16)
    m, ws, wo = m.astype(BF16), ws.astype(BF16), wo.astype(BF16)
    m = jnp.einsum("gab,gbq->gaq", m, perm, preferred_element_type=F32).astype(BF16)
    m = jnp.einsum("gap,gaq->gpq", perm, m, preferred_element_type=F32).astype(BF16)
    ws = jnp.einsum("gap,gan->gpn", perm, ws, preferred_element_type=F32).astype(BF16)
    wo = jnp.einsum("gna,gaq->gnq", wo, perm, preferred_element_type=F32).astype(BF16)
    return m, ws, wo, sc, d_t


def _glu_kernel(yg_ref, wa_ref, wb_ref, x_ref, o_ref, z_ref, lhs_ref):
    jb, pitch = REGROUP_CHUNKS, REGROUP_PITCH
    ntile = lhs_ref.shape[1] // LANES_V7X

    @pl.when(pl.program_id(1) == 0)
    def _():
        for g in range(yg_ref.shape[0]):
            t, g8 = divmod(g, GROUPS_PER_TILE)
            y = yg_ref[g].astype(F32)
            rows = slice(g8 * pitch, g8 * pitch + jb)
            z_ref[2 * t, rows, :] = y[:, :LANES_V7X]
            z_ref[2 * t + 1, rows, :] = y[:, LANES_V7X:]
        bits = _slot_bits()

        def body(j, _):
            r0 = pl.multiple_of(j * S5_T, S5_T)
            for t in range(ntile):
                halves = [_group_to_time_vreg(z_ref[2 * t + half, pl.ds(j, SUBLANES_V7X, stride=pitch), :], bits)
                          for half in range(2)]
                lhs_ref[pl.ds(r0, S5_T), t * LANES_V7X:(t + 1) * LANES_V7X] = (
                    jnp.concatenate(halves, axis=0).astype(lhs_ref.dtype))
            return 0

        lax.fori_loop(0, jb, body, 0)

    g = lhs_ref[...]
    a = jnp.dot(g, wa_ref[...], preferred_element_type=F32)
    b = jnp.dot(g, wb_ref[...], preferred_element_type=F32)
    o_ref[...] = x_ref[...] + a / (1.0 + jnp.exp(-b))


def _glu_residual(yg, w, layer, x, tn=512):
    ng, nchunk, width = yg.shape
    n, d = x.shape
    k = ng * S5_GROUP
    tm = REGROUP_CHUNKS * S5_T
    nj = d // tn
    return pl.pallas_call(
        _glu_kernel,
        grid=(n // tm, nj),
        in_specs=[pl.BlockSpec((ng, REGROUP_CHUNKS, width), lambda i, j: (0, i, 0)),
                  pl.BlockSpec((None, k, tn), lambda i, j: (layer, 0, j)),
                  pl.BlockSpec((None, k, tn), lambda i, j: (layer, 0, j + nj)),
                  pl.BlockSpec((tm, tn), lambda i, j: (i, j))],
        out_specs=pl.BlockSpec((tm, tn), lambda i, j: (i, j)),
        out_shape=jax.ShapeDtypeStruct((n, d), F32),
        scratch_shapes=[pltpu.VMEM((2 * k // LANES_V7X, GROUPS_PER_TILE * REGROUP_PITCH, LANES_V7X), F32),
                        pltpu.VMEM((tm, k), BF16)],
        compiler_params=_cparams(("parallel", "arbitrary")),
        name="glu_residual",
    )(yg, w, w, x)


def _linres_kernel(a_ref, w_ref, r_ref, o_ref):
    o_ref[...] = r_ref[...] + jnp.dot(a_ref[...], w_ref[...], preferred_element_type=F32)


def _linear_residual(a, w, layer, r, tm=1024, tn=512):
    n, k = a.shape
    d = w.shape[2]
    return pl.pallas_call(
        _linres_kernel,
        grid=(n // tm, d // tn),
        in_specs=[pl.BlockSpec((tm, k), lambda i, j: (i, 0)),
                  pl.BlockSpec((None, k, tn), lambda i, j: (layer, 0, j)),
                  pl.BlockSpec((tm, tn), lambda i, j: (i, j))],
        out_specs=pl.BlockSpec((tm, tn), lambda i, j: (i, j)),
        out_shape=jax.ShapeDtypeStruct((n, d), F32),
        compiler_params=_cparams(("parallel", "arbitrary")),
        name="linear_residual",
    )(a, w, r)


def _ffn_kernel(h_ref, gn_ref, wg_ref, wu_ref, wd_ref, gf_ref, o_ref, hn_ref, *, final_norm):
    j = pl.program_id(1)

    @pl.when(j == 0)
    def _():
        h = h_ref[...]
        hn_ref[...] = _rms(h, gn_ref[...]).astype(hn_ref.dtype)
        o_ref[...] = h

    hn = hn_ref[...]
    gate = jnp.dot(hn, wg_ref[...], preferred_element_type=F32)
    up = jnp.dot(hn, wu_ref[...], preferred_element_type=F32)
    act = (gate / (1.0 + jnp.exp(-gate)) * up).astype(BF16)
    o_ref[...] += jnp.dot(act, wd_ref[...], preferred_element_type=F32)

    if final_norm:
        @pl.when(j == pl.num_programs(1) - 1)
        def _():
            o_ref[...] = _rms(o_ref[...], gf_ref[...])


def _ffn(h, gain, wg, wu, wd, layer, gain_final, final_norm, tm=1024, tf=512):
    n, d = h.shape
    f = wg.shape[2]
    return pl.pallas_call(
        functools.partial(_ffn_kernel, final_norm=final_norm),
        grid=(n // tm, f // tf),
        in_specs=[pl.BlockSpec((tm, d), lambda i, j: (i, 0)),
                  pl.BlockSpec((1, d), lambda i, j: (0, 0)),
                  pl.BlockSpec((None, d, tf), lambda i, j: (layer, 0, j)),
                  pl.BlockSpec((None, d, tf), lambda i, j: (layer, 0, j)),
                  pl.BlockSpec((None, tf, d), lambda i, j: (layer, j, 0)),
                  pl.BlockSpec((1, d), lambda i, j: (0, 0))],
        out_specs=pl.BlockSpec((tm, d), lambda i, j: (i, 0)),
        out_shape=jax.ShapeDtypeStruct((n, d), F32),
        scratch_shapes=[pltpu.VMEM((tm, d), BF16)],
        compiler_params=_cparams(("parallel", "arbitrary")),
        name="ffn_final" if final_norm else "ffn",
    )(h, gain.reshape(1, d), wg, wu, wd, gain_final.reshape(1, d))


def _qkv_kernel(h_ref, gn_ref, w_ref, b_ref, o_ref, hn_ref):
    @pl.when(pl.program_id(1) == 0)
    def _():
        hn_ref[...] = _rms(h_ref[...], gn_ref[...]).astype(hn_ref.dtype)

    acc = jnp.dot(hn_ref[...], w_ref[...], preferred_element_type=F32)
    o_ref[...] = (acc + b_ref[...]).astype(o_ref.dtype)


def _qkv(h, gain, w, layer, b, tm=1024, tn=512):
    n, d = h.shape
    m = w.shape[2]
    return pl.pallas_call(
        _qkv_kernel,
        grid=(n // tm, m // tn),
        in_specs=[pl.BlockSpec((tm, d), lambda i, j: (i, 0)),
                  pl.BlockSpec((1, d), lambda i, j: (0, 0)),
                  pl.BlockSpec((None, d, tn), lambda i, j: (layer, 0, j)),
                  pl.BlockSpec((1, tn), lambda i, j: (0, j))],
        out_specs=pl.BlockSpec((tm, tn), lambda i, j: (i, j)),
        out_shape=jax.ShapeDtypeStruct((n, m), BF16),
        scratch_shapes=[pltpu.VMEM((tm, d), BF16)],
        compiler_params=_cparams(("parallel", "arbitrary")),
        name="qkv",
    )(h, gain.reshape(1, d), w, b.reshape(1, m))


def _attn_kernel(slope_ref, sink_ref, q_ref, kv_ref, kvp_ref, o_ref, kbuf_ref, *, tq, kv_heads):
    blk = ATTN_BLOCK
    kvw = kv_heads * HEAD_DIM
    step = pl.program_id(0)
    kbuf_ref[0:blk, :] = kvp_ref[...]
    kbuf_ref[blk:, :] = kv_ref[...]

    qi = lax.broadcasted_iota(jnp.int32, (blk, 2 * blk), 0)
    kj = lax.broadcasted_iota(jnp.int32, (blk, 2 * blk), 1)
    dist = qi + blk - kj
    neg_dist = jnp.where((dist >= 0) & (dist < blk), -dist.astype(F32), -jnp.inf)

    def body(b, _):
        r0 = pl.multiple_of(b * blk, blk)
        first = jnp.logical_and(step == 0, b == 0)
        nd = jnp.where(kj < jnp.where(first, blk, 0), -jnp.inf, neg_dist)
        for kvh in range(kv_heads):
            k = kbuf_ref[pl.ds(r0, 2 * blk), kvh * HEAD_DIM:(kvh + 1) * HEAD_DIM]
            v = kbuf_ref[pl.ds(r0, 2 * blk), kvw + kvh * HEAD_DIM:kvw + (kvh + 1) * HEAD_DIM]
            heads = [kvh * Q_PER_KV + g for g in range(Q_PER_KV)]
            q8 = jnp.concatenate(
                [q_ref[pl.ds(r0, blk), h * HEAD_DIM:(h + 1) * HEAD_DIM] for h in heads], axis=0)
            s8 = lax.dot_general(q8, k, (((1,), (1,)), ((), ())), preferred_element_type=F32)
            ps, dens = [], []
            for g, h in enumerate(heads):
                s = s8[g * blk:(g + 1) * blk] * (HEAD_DIM ** -0.5) + slope_ref[h] * nd
                sink = sink_ref[h]
                mx = jnp.maximum(jnp.max(s, axis=-1, keepdims=True), sink)
                p = jnp.exp(s - mx)
                dens.append(jnp.sum(p, axis=-1, keepdims=True) + jnp.exp(sink - mx))
                ps.append(p.astype(BF16))
            o8 = jnp.dot(jnp.concatenate(ps, axis=0), v, preferred_element_type=F32)
            outs = [o8[g * blk:(g + 1) * blk] / dens[g] for g in range(Q_PER_KV)]
            lo = kvh * Q_PER_KV * HEAD_DIM
            o_ref[pl.ds(r0, blk), lo:lo + Q_PER_KV * HEAD_DIM] = (
                jnp.concatenate(outs, axis=-1).astype(o_ref.dtype))
        return 0

    lax.fori_loop(0, tq // blk, body, 0)


def _attention(qkv, slopes, sinks, q_dim, kv_heads, tq=512):
    n = qkv.shape[0]
    kvw = 2 * kv_heads * HEAD_DIM
    kv_col = q_dim // kvw
    per = tq // ATTN_BLOCK
    return pl.pallas_call(
        functools.partial(_attn_kernel, tq=tq, kv_heads=kv_heads),
        grid=(n // tq,),
        in_specs=[pl.BlockSpec(memory_space=pltpu.SMEM),
                  pl.BlockSpec(memory_space=pltpu.SMEM),
                  pl.BlockSpec((tq, q_dim), lambda i: (i, 0)),
                  pl.BlockSpec((tq, kvw), lambda i: (i, kv_col)),
                  pl.BlockSpec((ATTN_BLOCK, kvw), lambda i: (jnp.maximum(i * per - 1, 0), kv_col))],
        out_specs=pl.BlockSpec((tq, q_dim), lambda i: (i, 0)),
        out_shape=jax.ShapeDtypeStruct((n, q_dim), BF16),
        scratch_shapes=[pltpu.VMEM((tq + ATTN_BLOCK, kvw), BF16)],
        compiler_params=_cparams(("parallel",)),
        name="swa_attention",
    )(slopes, sinks, qkv, qkv, qkv)


def kernel(x, norm_mix, s5_a_re, s5_a_im, s5_log_step, s5_b_re, s5_b_im, s5_c_re, s5_c_im, s5_d, s5_w_glu, attn_w_qkv, attn_b_qkv, attn_sinks, attn_w_o, norm_ffn, ffn_w_gate, ffn_w_up, ffn_w_down, norm_final):
    bsz, seq, dm = x.shape
    ng = dm // S5_GROUP
    nchunk = seq // S5_T
    n_heads = attn_sinks.shape[1]
    q_dim = n_heads * HEAD_DIM
    kv_heads = n_heads // Q_PER_KV
    assert bsz == 1 and norm_mix.shape[0] == 2 and seq % 1024 == 0
    assert S5_T * S5_GROUP == 2 * LANES_V7X and s5_a_re.shape[2] == S5_STATE
    assert q_dim % (2 * kv_heads * HEAD_DIM) == 0

    h0 = x.reshape(seq, dm)

    w_gate, w_up, w_down = (w.astype(BF16) for w in (ffn_w_gate, ffn_w_up, ffn_w_down))
    xg = _norm_regroup(h0, norm_mix[0])
    tables = _s5_tables(s5_a_re[0], s5_a_im[0], s5_log_step[0], s5_b_re[0], s5_b_im[0],
                        s5_c_re[0], s5_c_im[0], s5_d[0])
    yg = _s5_core(xg, *tables)
    h1 = _glu_residual(yg, s5_w_glu.astype(BF16), 0, h0)
    h2 = _ffn(h1, norm_ffn[0], w_gate, w_up, w_down, 0, norm_final, final_norm=False)

    qkv = _qkv(h2, norm_mix[1], attn_w_qkv.astype(BF16), 0, attn_b_qkv[0])
    slopes = jnp.exp2(-8.0 * jnp.arange(1, n_heads + 1, dtype=F32) / n_heads)
    o = _attention(qkv, slopes, attn_sinks[0], q_dim, kv_heads)
    h3 = _linear_residual(o, attn_w_o.astype(BF16), 0, h2)
    out = _ffn(h3, norm_ffn[1], w_gate, w_up, w_down, 1, norm_final, final_norm=True)
    return out.reshape(bsz, seq, dm)
```

```python
import functools
import math

import jax
import jax.numpy as jnp
from jax import lax
from jax.experimental import pallas as pl
from jax.experimental.pallas import tpu as pltpu

F32 = jnp.float32
BF16 = jnp.bfloat16

NORM_EPS = 1e-6
S5_GROUP = 16
S5_STATE = 64
S5_T = 16
HEAD_DIM = 64
Q_PER_KV = 8
ATTN_BLOCK = 128
LANES_V7X = 128
SUBLANES_V7X = 8
VMEM_LIMIT_V7X = 56 * 1024 * 1024


def _cparams(sem):
    return pltpu.CompilerParams(dimension_semantics=sem, vmem_limit_bytes=VMEM_LIMIT_V7X)


def _rms(h, gain):
    return h * lax.rsqrt(jnp.mean(h * h, axis=-1, keepdims=True) + NORM_EPS) * gain


GROUPS_PER_TILE = LANES_V7X // S5_GROUP
REGROUP_CHUNKS = 64
REGROUP_PITCH = REGROUP_CHUNKS + 8
REGROUP_UNROLL = 4


def _slot_bits():
    shape = (SUBLANES_V7X, LANES_V7X)
    sub = lax.broadcasted_iota(jnp.int32, shape, 0)
    lane = lax.broadcasted_iota(jnp.int32, shape, 1)
    return ([((sub >> b) & 1) == 1 for b in range(3)],
            [((lane >> (4 + b)) & 1) == 1 for b in range(3)])


def _pack_bf16_pair(a, b):
    ua = pltpu.bitcast(a.astype(BF16).astype(F32), jnp.uint32)
    ub = pltpu.bitcast(b.astype(BF16).astype(F32), jnp.uint32)
    return ua | (ub >> 16)


def _unpack_bf16_pair(u):
    return pltpu.bitcast(u & jnp.uint32(0xFFFF0000), F32), pltpu.bitcast(u << 16, F32)


def _time_to_group_vreg(x, bits):
    sub_bits, slot_bits = bits
    for b in range(3):
        x = jnp.where(sub_bits[b], pltpu.roll(x, LANES_V7X - (S5_GROUP << b), 1), x)
    for b in range(3):
        x = jnp.where(slot_bits[b], pltpu.roll(x, 1 << b, 0), x)
    return x


def _group_to_time_vreg(z, bits):
    sub_bits, slot_bits = bits
    for b in range(3):
        z = jnp.where(slot_bits[b], pltpu.roll(z, SUBLANES_V7X - (1 << b), 0), z)
    for b in range(3):
        z = jnp.where(sub_bits[b], pltpu.roll(z, S5_GROUP << b, 1), z)
    return z


def _norm_regroup_kernel(h_ref, g_ref, o_ref, inv_ref, z_ref):
    jb, pitch = REGROUP_CHUNKS, REGROUP_PITCH
    ntile = h_ref.shape[1] // LANES_V7X
    h = h_ref[...]
    inv = lax.rsqrt(jnp.mean(h * h, axis=-1, keepdims=True) + NORM_EPS)
    inv_ref[...] = jnp.broadcast_to(inv, inv_ref.shape)
    bits = _slot_bits()

    def body(j, _):
        r0 = pl.multiple_of(j * S5_T, S5_T)
        r1 = pl.multiple_of(j * S5_T + SUBLANES_V7X, SUBLANES_V7X)
        scale0 = inv_ref[pl.ds(r0, SUBLANES_V7X), :]
        scale1 = inv_ref[pl.ds(r1, SUBLANES_V7X), :]
        for t in range(ntile):
            lanes = slice(t * LANES_V7X, (t + 1) * LANES_V7X)
            gain = g_ref[:, lanes]
            x0 = h_ref[pl.ds(r0, SUBLANES_V7X), lanes] * scale0 * gain
            x1 = h_ref[pl.ds(r1, SUBLANES_V7X), lanes] * scale1 * gain
            z_ref[t, pl.ds(j, SUBLANES_V7X, stride=pitch), :] = _time_to_group_vreg(
                _pack_bf16_pair(x0, x1), bits)
        return 0

    lax.fori_loop(0, jb, body, 0, unroll=REGROUP_UNROLL)
    for g in range(o_ref.shape[0]):
        t, g8 = divmod(g, GROUPS_PER_TILE)
        halves = _unpack_bf16_pair(z_ref[t, g8 * pitch:g8 * pitch + jb, :])
        o_ref[g] = jnp.concatenate(halves, axis=-1).astype(o_ref.dtype)


def _norm_regroup(h, gain):
    n, d = h.shape
    tm = REGROUP_CHUNKS * S5_T
    ng = d // S5_GROUP
    return pl.pallas_call(
        _norm_regroup_kernel,
        grid=(n // tm,),
        in_specs=[pl.BlockSpec((tm, d), lambda i: (i, 0)),
                  pl.BlockSpec((1, d), lambda i: (0, 0))],
        out_specs=pl.BlockSpec((ng, REGROUP_CHUNKS, S5_T * S5_GROUP), lambda i: (0, i, 0)),
        out_shape=jax.ShapeDtypeStruct((ng, n // S5_T, S5_T * S5_GROUP), BF16),
        scratch_shapes=[pltpu.VMEM((tm, LANES_V7X), F32),
                        pltpu.VMEM((d // LANES_V7X, GROUPS_PER_TILE * REGROUP_PITCH, LANES_V7X), jnp.uint32)],
        compiler_params=_cparams(("parallel",)),
        name="norm_regroup",
    )(h, gain.reshape(1, d))


def _s5_kernel(x_ref, m_ref, ws_ref, wo_ref, sc_ref, d_ref, o_ref, s_ref, h_ref, *, gb, nchunk):
    for g in range(gb):
        s_ref[g] = jnp.dot(x_ref[g], ws_ref[g], preferred_element_type=F32)

    row = lax.broadcasted_iota(jnp.int32, (SUBLANES_V7X, LANES_V7X), 0)
    zero = jnp.zeros((SUBLANES_V7X, LANES_V7X), F32)

    def body(i, carry):
        r0 = pl.multiple_of(i * SUBLANES_V7X, SUBLANES_V7X)
        out = []
        for g in range(gb):
            cb, cbs = carry[g]
            z = s_ref[g, pl.ds(r0, SUBLANES_V7X), 0:LANES_V7X]
            zs = s_ref[g, pl.ds(r0, SUBLANES_V7X), LANES_V7X:2 * LANES_V7X]
            for qi, q in enumerate((1, 2, 4)):
                a1 = sc_ref[g, 16 * qi:16 * qi + 8, :]
                a2 = sc_ref[g, 16 * qi + 8:16 * qi + 16, :]
                zr = pltpu.roll(z, q, axis=0)
                zsr = pltpu.roll(zs, q, axis=0)
                z, zs = z + a1 * zr + a2 * zsr, zs + a1 * zsr - a2 * zr
            p1 = sc_ref[g, 48:56, :]
            p2 = sc_ref[g, 56:64, :]
            hinc = z + p1 * cb + p2 * cbs
            hincs = zs + p1 * cbs - p2 * cb
            h_ref[g, pl.ds(r0, SUBLANES_V7X), :] = jnp.where(row == 0, cb, pltpu.roll(hinc, 1, axis=0))
            out.append((jnp.broadcast_to(hinc[7:8, :], (SUBLANES_V7X, LANES_V7X)),
                        jnp.broadcast_to(hincs[7:8, :], (SUBLANES_V7X, LANES_V7X))))
        return tuple(out)

    lax.fori_loop(0, nchunk // SUBLANES_V7X, body, tuple((zero, zero) for _ in range(gb)))

    for g in range(gb):
        x = x_ref[g]
        y = jnp.dot(x, m_ref[g], preferred_element_type=F32)
        y = y + jnp.dot(h_ref[g].astype(BF16), wo_ref[g], preferred_element_type=F32)
        y = y + d_ref[g] * x.astype(F32)
        o_ref[g] = jax.nn.gelu(y).astype(o_ref.dtype)


def _s5_core(xg, m, ws, wo, sc, dt, gb=4):
    ng, nchunk, w = xg.shape
    spec3 = lambda a, b: pl.BlockSpec((gb, a, b), lambda i: (i, 0, 0))
    return pl.pallas_call(
        functools.partial(_s5_kernel, gb=gb, nchunk=nchunk),
        grid=(ng // gb,),
        in_specs=[spec3(nchunk, w), spec3(w, w), spec3(w, w), spec3(2 * S5_STATE, w),
                  spec3(64, LANES_V7X), spec3(1, w)],
        out_specs=spec3(nchunk, w),
        out_shape=jax.ShapeDtypeStruct((ng, nchunk, w), BF16),
        scratch_shapes=[pltpu.VMEM((gb, nchunk, w), F32),
                        pltpu.VMEM((gb, nchunk, 2 * S5_STATE), F32)],
        compiler_params=_cparams(("parallel",)),
        name="s5_core",
    )(xg, m, ws, wo, sc, dt)


def _s5_tables(a_re, a_im, log_step, b_re, b_im, c_re, c_im, d_skip):
    ng, ns = a_re.shape
    t = S5_T
    dt = jnp.exp(log_step)[:, None, None]
    lr, li = a_re[:, None, :], a_im[:, None, :]

    def powers(e):
        ee = e.astype(F32)[None, :, None]
        mag = jnp.exp(lr * dt * ee)
        ang = li * dt * ee
        return mag * jnp.cos(ang), mag * jnp.sin(ang)

    lb_re, lb_im = powers(jnp.arange(1, 2))
    n_re, n_im = lb_re - 1.0, lb_im
    den = lr * lr + li * li
    q_re = ((n_re * lr + n_im * li) / den)[:, 0, :, None]
    q_im = ((n_im * lr - n_re * li) / den)[:, 0, :, None]
    bb_re = q_re * b_re - q_im * b_im
    bb_im = q_re * b_im + q_im * b_re

    def c_times(pr, pi):
        cr, ci = c_re[:, None], c_im[:, None]
        pr, pi = pr[:, :, None, :], pi[:, :, None, :]
        return cr * pr - ci * pi, cr * pi + ci * pr

    ca_re, ca_im = c_times(*powers(jnp.arange(0, t)))
    bt_re, bt_im = bb_re.transpose(0, 2, 1)[:, None, :, None], bb_im.transpose(0, 2, 1)[:, None, :, None]
    kern = jnp.sum(ca_re[:, :, None] * bt_re - ca_im[:, :, None] * bt_im, axis=-1)
    lag = jnp.arange(t)[None, :] - jnp.arange(t)[:, None]
    m = jnp.where((lag >= 0)[None, :, :, None, None], kern[:, jnp.clip(lag, 0, t - 1)], 0.0)
    m = m.transpose(0, 1, 3, 2, 4).reshape(ng, t * S5_GROUP, t * S5_GROUP)

    pr, pi = powers(jnp.arange(t - 1, -1, -1))
    pr, pi = pr[:, :, None, :], pi[:, :, None, :]
    br, bi = bb_re.transpose(0, 2, 1)[:, None], bb_im.transpose(0, 2, 1)[:, None]
    ws_re, ws_im = pr * br - pi * bi, pr * bi + pi * br
    ws = jnp.concatenate([ws_re, ws_im, ws_im, ws_re], axis=-1).reshape(ng, t * S5_GROUP, 4 * ns)

    co_re, co_im = c_times(*powers(jnp.arange(1, t + 1)))
    wo = jnp.concatenate([co_re.transpose(0, 3, 1, 2), -co_im.transpose(0, 3, 1, 2)], axis=1)
    wo = wo.reshape(ng, 2 * ns, t * S5_GROUP)

    def pair(e, rows_from):
        er, ei = powers(e)
        keep = (jnp.arange(SUBLANES_V7X) >= rows_from)[None, :, None]
        a1 = jnp.where(keep, jnp.concatenate([er, er], axis=-1), 0.0)
        a2 = jnp.where(keep, jnp.concatenate([-ei, ei], axis=-1), 0.0)
        return [a1, a2]

    sc = []
    for q in (1, 2, 4):
        sc += pair(jnp.full((SUBLANES_V7X,), t * q), q)
    sc += pair(t * jnp.arange(1, SUBLANES_V7X + 1), 0)
    sc = jnp.concatenate(sc, axis=1)

    d_t = jnp.tile(d_skip.reshape(ng, 1, S5_GROUP), (1, 1, t))

    r = jnp.arange(t * S5_GROUP)
    half, slot, chan = r // LANES_V7X, (r % LANES_V7X) // S5_GROUP, r % S5_GROUP
    g8 = jnp.arange(ng)[:, None] % GROUPS_PER_TILE
    step = SUBLANES_V7X * half[None] + (g8 - slot[None]) % SUBLANES_V7X
    idx = step * S5_GROUP + chan[None]
    perm = (idx[:, None, :] == r[None, :, None]).astype(BF16)
    m, ws, wo = m.astype(BF16), ws.astype(BF16), wo.astype(BF16)
    m = jnp.einsum("gab,gbq->gaq", m, perm, preferred_element_type=F32).astype(BF16)
    m = jnp.einsum("gap,gaq->gpq", perm, m, preferred_element_type=F32).astype(BF16)
    ws = jnp.einsum("gap,gan->gpn", perm, ws, preferred_element_type=F32).astype(BF16)
    wo = jnp.einsum("gna,gaq->gnq", wo, perm, preferred_element_type=F32).astype(BF16)
    return m, ws, wo, sc, d_t


def _glu_kernel(yg_ref, wa_ref, wb_ref, x_ref, o_ref, z_ref, lhs_ref):
    jb, pitch = REGROUP_CHUNKS, REGROUP_PITCH
    ntile = lhs_ref.shape[1] // LANES_V7X

    @pl.when(pl.program_id(1) == 0)
    def _():
        for g in range(yg_ref.shape[0]):
            t, g8 = divmod(g, GROUPS_PER_TILE)
            y = yg_ref[g].astype(F32)
            z_ref[t, g8 * pitch:g8 * pitch + jb, :] = (
                pltpu.bitcast(y[:, :LANES_V7X], jnp.uint32) | (pltpu.bitcast(y[:, LANES_V7X:], jnp.uint32) >> 16))
        bits = _slot_bits()

        def body(j, _):
            r0 = pl.multiple_of(j * S5_T, S5_T)
            for t in range(ntile):
                halves = _unpack_bf16_pair(
                    _group_to_time_vreg(z_ref[t, pl.ds(j, SUBLANES_V7X, stride=pitch), :], bits))
                lhs_ref[pl.ds(r0, S5_T), t * LANES_V7X:(t + 1) * LANES_V7X] = (
                    jnp.concatenate(halves, axis=0).astype(lhs_ref.dtype))
            return 0

        lax.fori_loop(0, jb, body, 0, unroll=REGROUP_UNROLL)

    g = lhs_ref[...]
    a = jnp.dot(g, wa_ref[...], preferred_element_type=F32)
    b = jnp.dot(g, wb_ref[...], preferred_element_type=F32)
    o_ref[...] = x_ref[...] + a / (1.0 + jnp.exp(-b))


def _glu_residual(yg, w, layer, x, tn=512):
    ng, nchunk, width = yg.shape
    n, d = x.shape
    k = ng * S5_GROUP
    tm = REGROUP_CHUNKS * S5_T
    nj = d // tn
    return pl.pallas_call(
        _glu_kernel,
        grid=(n // tm, nj),
        in_specs=[pl.BlockSpec((ng, REGROUP_CHUNKS, width), lambda i, j: (0, i, 0)),
                  pl.BlockSpec((None, k, tn), lambda i, j: (layer, 0, j)),
                  pl.BlockSpec((None, k, tn), lambda i, j: (layer, 0, j + nj)),
                  pl.BlockSpec((tm, tn), lambda i, j: (i, j))],
        out_specs=pl.BlockSpec((tm, tn), lambda i, j: (i, j)),
        out_shape=jax.ShapeDtypeStruct((n, d), F32),
        scratch_shapes=[pltpu.VMEM((k // LANES_V7X, GROUPS_PER_TILE * REGROUP_PITCH, LANES_V7X), jnp.uint32),
                        pltpu.VMEM((tm, k), BF16)],
        compiler_params=_cparams(("parallel", "arbitrary")),
        name="glu_residual",
    )(yg, w, w, x)


def _linres_kernel(a_ref, w_ref, r_ref, o_ref):
    o_ref[...] = r_ref[...] + jnp.dot(a_ref[...], w_ref[...], preferred_element_type=F32)


def _linear_residual(a, w, layer, r, tm=1024, tn=2048):
    n, k = a.shape
    d = w.shape[2]
    return pl.pallas_call(
        _linres_kernel,
        grid=(n // tm, d // tn),
        in_specs=[pl.BlockSpec((tm, k), lambda i, j: (i, 0)),
                  pl.BlockSpec((None, k, tn), lambda i, j: (layer, 0, j)),
                  pl.BlockSpec((tm, tn), lambda i, j: (i, j))],
        out_specs=pl.BlockSpec((tm, tn), lambda i, j: (i, j)),
        out_shape=jax.ShapeDtypeStruct((n, d), F32),
        compiler_params=_cparams(("parallel", "arbitrary")),
        name="linear_residual",
    )(a, w, r)


def _ffn_kernel(h_ref, gn_ref, wg_ref, wu_ref, wd_ref, gf_ref, o_ref, hn_ref, *, final_norm):
    j = pl.program_id(1)

    @pl.when(j == 0)
    def _():
        h = h_ref[...]
        hn_ref[...] = _rms(h, gn_ref[...]).astype(hn_ref.dtype)
        o_ref[...] = h

    hn = hn_ref[...]
    gate = jnp.dot(hn, wg_ref[...], preferred_element_type=F32)
    up = jnp.dot(hn, wu_ref[...], preferred_element_type=F32)
    act = (gate / (1.0 + jnp.exp(-gate)) * up).astype(BF16)
    o_ref[...] += jnp.dot(act, wd_ref[...], preferred_element_type=F32)

    if final_norm:
        @pl.when(j == pl.num_programs(1) - 1)
        def _():
            o_ref[...] = _rms(o_ref[...], gf_ref[...])


def _ffn(h, gain, wg, wu, wd, layer, gain_final, final_norm, tm=1024, tf=512):
    n, d = h.shape
    f = wg.shape[2]
    return pl.pallas_call(
        functools.partial(_ffn_kernel, final_norm=final_norm),
        grid=(n // tm, f // tf),
        in_specs=[pl.BlockSpec((tm, d), lambda i, j: (i, 0)),
                  pl.BlockSpec((1, d), lambda i, j: (0, 0)),
                  pl.BlockSpec((None, d, tf), lambda i, j: (layer, 0, j)),
                  pl.BlockSpec((None, d, tf), lambda i, j: (layer, 0, j)),
                  pl.BlockSpec((None, tf, d), lambda i, j: (layer, j, 0)),
                  pl.BlockSpec((1, d), lambda i, j: (0, 0))],
        out_specs=pl.BlockSpec((tm, d), lambda i, j: (i, 0)),
        out_shape=jax.ShapeDtypeStruct((n, d), F32),
        scratch_shapes=[pltpu.VMEM((tm, d), BF16)],
        compiler_params=_cparams(("parallel", "arbitrary")),
        name="ffn_final" if final_norm else "ffn",
    )(h, gain.reshape(1, d), wg, wu, wd, gain_final.reshape(1, d))


def _qkv_kernel(h_ref, gn_ref, w_ref, b_ref, o_ref, hn_ref):
    @pl.when(pl.program_id(1) == 0)
    def _():
        hn_ref[...] = _rms(h_ref[...], gn_ref[...]).astype(hn_ref.dtype)

    acc = jnp.dot(hn_ref[...], w_ref[...], preferred_element_type=F32)
    o_ref[...] = (acc + b_ref[...]).astype(o_ref.dtype)


def _qkv(h, gain, w, layer, b, tm=1024, tn=2560):
    n, d = h.shape
    m = w.shape[2]
    return pl.pallas_call(
        _qkv_kernel,
        grid=(n // tm, m // tn),
        in_specs=[pl.BlockSpec((tm, d), lambda i, j: (i, 0)),
                  pl.BlockSpec((1, d), lambda i, j: (0, 0)),
                  pl.BlockSpec((None, d, tn), lambda i, j: (layer, 0, j)),
                  pl.BlockSpec((1, tn), lambda i, j: (0, j))],
        out_specs=pl.BlockSpec((tm, tn), lambda i, j: (i, j)),
        out_shape=jax.ShapeDtypeStruct((n, m), BF16),
        scratch_shapes=[pltpu.VMEM((tm, d), BF16)],
        compiler_params=_cparams(("parallel", "arbitrary")),
        name="qkv",
    )(h, gain.reshape(1, d), w, b.reshape(1, m))


def _attn_kernel(slope_ref, sink_ref, q_ref, kv_ref, kvp_ref, o_ref, kbuf_ref, *, tq, kv_heads):
    blk = ATTN_BLOCK
    kvw = kv_heads * HEAD_DIM
    step = pl.program_id(0)
    kbuf_ref[0:blk, :] = kvp_ref[...]
    kbuf_ref[blk:, :] = kv_ref[...]

    qi = lax.broadcasted_iota(jnp.int32, (blk, 2 * blk), 0)
    kj = lax.broadcasted_iota(jnp.int32, (blk, 2 * blk), 1)
    dist = qi + blk - kj
    neg_dist = jnp.where((dist >= 0) & (dist < blk), -dist.astype(F32), -jnp.inf)

    def body(b, _):
        r0 = pl.multiple_of(b * blk, blk)
        first = jnp.logical_and(step == 0, b == 0)
        nd = jnp.where(kj < jnp.where(first, blk, 0), -jnp.inf, neg_dist)
        for kvh in range(kv_heads):
            k = kbuf_ref[pl.ds(r0, 2 * blk), kvh * HEAD_DIM:(kvh + 1) * HEAD_DIM]
            v = kbuf_ref[pl.ds(r0, 2 * blk), kvw + kvh * HEAD_DIM:kvw + (kvh + 1) * HEAD_DIM]
            heads = [kvh * Q_PER_KV + g for g in range(Q_PER_KV)]
            q8 = jnp.concatenate(
                [q_ref[pl.ds(r0, blk), h * HEAD_DIM:(h + 1) * HEAD_DIM] for h in heads], axis=0)
            s8 = lax.dot_general(q8, k, (((1,), (1,)), ((), ())), preferred_element_type=F32)
            ps, dens = [], []
            for g, h in enumerate(heads):
                s = s8[g * blk:(g + 1) * blk] * (HEAD_DIM ** -0.5) + slope_ref[h] * nd
                sink = sink_ref[h]
                mx = jnp.maximum(jnp.max(s, axis=-1, keepdims=True), sink)
                p = jnp.exp(s - mx)
                dens.append(jnp.sum(p, axis=-1, keepdims=True) + jnp.exp(sink - mx))
                ps.append(p.astype(BF16))
            o8 = jnp.dot(jnp.concatenate(ps, axis=0), v, preferred_element_type=F32)
            outs = [o8[g * blk:(g + 1) * blk] / dens[g] for g in range(Q_PER_KV)]
            lo = kvh * Q_PER_KV * HEAD_DIM
            o_ref[pl.ds(r0, blk), lo:lo + Q_PER_KV * HEAD_DIM] = (
                jnp.concatenate(outs, axis=-1).astype(o_ref.dtype))
        return 0

    lax.fori_loop(0, tq // blk, body, 0)


def _attention(qkv, slopes, sinks, q_dim, kv_heads, tq=512):
    n = qkv.shape[0]
    kvw = 2 * kv_heads * HEAD_DIM
    kv_col = q_dim // kvw
    per = tq // ATTN_BLOCK
    return pl.pallas_call(
        functools.partial(_attn_kernel, tq=tq, kv_heads=kv_heads),
        grid=(n // tq,),
        in_specs=[pl.BlockSpec(memory_space=pltpu.SMEM),
                  pl.BlockSpec(memory_space=pltpu.SMEM),
                  pl.BlockSpec((tq, q_dim), lambda i: (i, 0)),
                  pl.BlockSpec((tq, kvw), lambda i: (i, kv_col)),
                  pl.BlockSpec((ATTN_BLOCK, kvw), lambda i: (jnp.maximum(i * per - 1, 0), kv_col))],
        out_specs=pl.BlockSpec((tq, q_dim), lambda i: (i, 0)),
        out_shape=jax.ShapeDtypeStruct((n, q_dim), BF16),
        scratch_shapes=[pltpu.VMEM((tq + ATTN_BLOCK, kvw), BF16)],
        compiler_params=_cparams(("parallel",)),
        name="swa_attention",
    )(slopes, sinks, qkv, qkv, qkv)


def kernel(x, norm_mix, s5_a_re, s5_a_im, s5_log_step, s5_b_re, s5_b_im, s5_c_re, s5_c_im, s5_d, s5_w_glu, attn_w_qkv, attn_b_qkv, attn_sinks, attn_w_o, norm_ffn, ffn_w_gate, ffn_w_up, ffn_w_down, norm_final):
    bsz, seq, dm = x.shape
    ng = dm // S5_GROUP
    nchunk = seq // S5_T
    n_heads = attn_sinks.shape[1]
    q_dim = n_heads * HEAD_DIM
    kv_heads = n_heads // Q_PER_KV
    assert bsz == 1 and norm_mix.shape[0] == 2 and seq % 1024 == 0
    assert S5_T * S5_GROUP == 2 * LANES_V7X and s5_a_re.shape[2] == S5_STATE
    assert q_dim % (2 * kv_heads * HEAD_DIM) == 0

    h0 = x.reshape(seq, dm)

    w_gate, w_up, w_down = (w.astype(BF16) for w in (ffn_w_gate, ffn_w_up, ffn_w_down))
    xg = _norm_regroup(h0, norm_mix[0])
    tables = _s5_tables(s5_a_re[0], s5_a_im[0], s5_log_step[0], s5_b_re[0], s5_b_im[0],
                        s5_c_re[0], s5_c_im[0], s5_d[0])
    yg = _s5_core(xg, *tables)
    h1 = _glu_residual(yg, s5_w_glu.astype(BF16), 0, h0)
    h2 = _ffn(h1, norm_ffn[0], w_gate, w_up, w_down, 0, norm_final, final_norm=False)

    qkv = _qkv(h2, norm_mix[1], attn_w_qkv.astype(BF16), 0, attn_b_qkv[0])
    slopes = jnp.exp2(-8.0 * jnp.arange(1, n_heads + 1, dtype=F32) / n_heads)
    o = _attention(qkv, slopes, attn_sinks[0], q_dim, kv_heads)
    h3 = _linear_residual(o, attn_w_o.astype(BF16), 0, h2)
    out = _ffn(h3, norm_ffn[1], w_gate, w_up, w_down, 1, norm_final, final_norm=True)
    return out.reshape(bsz, seq, dm)
```

```python
import functools
import math

import jax
import jax.numpy as jnp
from jax import lax
from jax.experimental import pallas as pl
from jax.experimental.pallas import tpu as pltpu

F32 = jnp.float32
BF16 = jnp.bfloat16

NORM_EPS = 1e-6
S5_GROUP = 16
S5_STATE = 64
S5_T = 16
HEAD_DIM = 64
Q_PER_KV = 8
ATTN_BLOCK = 128
LANES_V7X = 128
SUBLANES_V7X = 8
VMEM_LIMIT_V7X = 56 * 1024 * 1024


def _cparams(sem):
    return pltpu.CompilerParams(dimension_semantics=sem, vmem_limit_bytes=VMEM_LIMIT_V7X)


def _rms(h, gain):
    return h * lax.rsqrt(jnp.mean(h * h, axis=-1, keepdims=True) + NORM_EPS) * gain


GROUPS_PER_TILE = LANES_V7X // S5_GROUP
REGROUP_CHUNKS = 64
REGROUP_PITCH = REGROUP_CHUNKS + 8
REGROUP_UNROLL = 2


def _slot_bits():
    shape = (SUBLANES_V7X, LANES_V7X)
    sub = lax.broadcasted_iota(jnp.int32, shape, 0)
    lane = lax.broadcasted_iota(jnp.int32, shape, 1)
    return ([((sub >> b) & 1) == 1 for b in range(3)],
            [((lane >> (4 + b)) & 1) == 1 for b in range(3)])


def _time_to_group_vreg(x, bits):
    sub_bits, slot_bits = bits
    for b in range(3):
        x = jnp.where(sub_bits[b], pltpu.roll(x, LANES_V7X - (S5_GROUP << b), 1), x)
    for b in range(3):
        x = jnp.where(slot_bits[b], pltpu.roll(x, 1 << b, 0), x)
    return x


def _group_to_time_vreg(z, bits):
    sub_bits, slot_bits = bits
    for b in range(3):
        z = jnp.where(slot_bits[b], pltpu.roll(z, SUBLANES_V7X - (1 << b), 0), z)
    for b in range(3):
        z = jnp.where(sub_bits[b], pltpu.roll(z, S5_GROUP << b, 1), z)
    return z


def _norm_regroup_kernel(h_ref, g_ref, o_ref, inv_ref, z_ref):
    jb, pitch = REGROUP_CHUNKS, REGROUP_PITCH
    ntile = h_ref.shape[1] // LANES_V7X
    h = h_ref[...]
    inv = lax.rsqrt(jnp.mean(h * h, axis=-1, keepdims=True) + NORM_EPS)
    inv_ref[...] = jnp.broadcast_to(inv, inv_ref.shape)
    bits = _slot_bits()

    def body(j, _):
        for half in range(2):
            r0 = pl.multiple_of(j * S5_T + half * SUBLANES_V7X, SUBLANES_V7X)
            scale = inv_ref[pl.ds(r0, SUBLANES_V7X), :]
            for t in range(ntile):
                lanes = slice(t * LANES_V7X, (t + 1) * LANES_V7X)
                x = h_ref[pl.ds(r0, SUBLANES_V7X), lanes] * scale * g_ref[:, lanes]
                z_ref[2 * t + half, pl.ds(j, SUBLANES_V7X, stride=pitch), :] = _time_to_group_vreg(x, bits)
        return 0

    lax.fori_loop(0, jb, body, 0, unroll=REGROUP_UNROLL)
    for g in range(o_ref.shape[0]):
        t, g8 = divmod(g, GROUPS_PER_TILE)
        rows = slice(g8 * pitch, g8 * pitch + jb)
        o_ref[g] = jnp.concatenate([z_ref[2 * t, rows, :], z_ref[2 * t + 1, rows, :]],
                                   axis=-1).astype(o_ref.dtype)


def _norm_regroup(h, gain):
    n, d = h.shape
    tm = REGROUP_CHUNKS * S5_T
    ng = d // S5_GROUP
    return pl.pallas_call(
        _norm_regroup_kernel,
        grid=(n // tm,),
        in_specs=[pl.BlockSpec((tm, d), lambda i: (i, 0)),
                  pl.BlockSpec((1, d), lambda i: (0, 0))],
        out_specs=pl.BlockSpec((ng, REGROUP_CHUNKS, S5_T * S5_GROUP), lambda i: (0, i, 0)),
        out_shape=jax.ShapeDtypeStruct((ng, n // S5_T, S5_T * S5_GROUP), BF16),
        scratch_shapes=[pltpu.VMEM((tm, LANES_V7X), F32),
                        pltpu.VMEM((2 * d // LANES_V7X, GROUPS_PER_TILE * REGROUP_PITCH, LANES_V7X), F32)],
        compiler_params=_cparams(("parallel",)),
        name="norm_regroup",
    )(h, gain.reshape(1, d))


def _s5_kernel(x_ref, m_ref, ws_ref, wo_ref, sc_ref, d_ref, o_ref, s_ref, h_ref, *, gb, nchunk):
    for g in range(gb):
        s_ref[g] = jnp.dot(x_ref[g], ws_ref[g], preferred_element_type=F32)

    row = lax.broadcasted_iota(jnp.int32, (SUBLANES_V7X, LANES_V7X), 0)
    zero = jnp.zeros((SUBLANES_V7X, LANES_V7X), F32)

    def body(i, carry):
        r0 = pl.multiple_of(i * SUBLANES_V7X, SUBLANES_V7X)
        out = []
        for g in range(gb):
            cb, cbs = carry[g]
            z = s_ref[g, pl.ds(r0, SUBLANES_V7X), 0:LANES_V7X]
            zs = s_ref[g, pl.ds(r0, SUBLANES_V7X), LANES_V7X:2 * LANES_V7X]
            for qi, q in enumerate((1, 2, 4)):
                a1 = sc_ref[g, 16 * qi:16 * qi + 8, :]
                a2 = sc_ref[g, 16 * qi + 8:16 * qi + 16, :]
                zr = pltpu.roll(z, q, axis=0)
                zsr = pltpu.roll(zs, q, axis=0)
                z, zs = z + a1 * zr + a2 * zsr, zs + a1 * zsr - a2 * zr
            p1 = sc_ref[g, 48:56, :]
            p2 = sc_ref[g, 56:64, :]
            hinc = z + p1 * cb + p2 * cbs
            hincs = zs + p1 * cbs - p2 * cb
            h_ref[g, pl.ds(r0, SUBLANES_V7X), :] = jnp.where(row == 0, cb, pltpu.roll(hinc, 1, axis=0))
            out.append((jnp.broadcast_to(hinc[7:8, :], (SUBLANES_V7X, LANES_V7X)),
                        jnp.broadcast_to(hincs[7:8, :], (SUBLANES_V7X, LANES_V7X))))
        return tuple(out)

    lax.fori_loop(0, nchunk // SUBLANES_V7X, body, tuple((zero, zero) for _ in range(gb)))

    for g in range(gb):
        x = x_ref[g]
        y = jnp.dot(x, m_ref[g], preferred_element_type=F32)
        y = y + jnp.dot(h_ref[g].astype(BF16), wo_ref[g], preferred_element_type=F32)
        y = y + d_ref[g] * x.astype(F32)
        o_ref[g] = jax.nn.gelu(y).astype(o_ref.dtype)


def _s5_core(xg, m, ws, wo, sc, dt, gb=8):
    ng, nchunk, w = xg.shape
    spec3 = lambda a, b: pl.BlockSpec((gb, a, b), lambda i: (i, 0, 0))
    return pl.pallas_call(
        functools.partial(_s5_kernel, gb=gb, nchunk=nchunk),
        grid=(ng // gb,),
        in_specs=[spec3(nchunk, w), spec3(w, w), spec3(w, w), spec3(2 * S5_STATE, w),
                  spec3(64, LANES_V7X), spec3(1, w)],
        out_specs=spec3(nchunk, w),
        out_shape=jax.ShapeDtypeStruct((ng, nchunk, w), BF16),
        scratch_shapes=[pltpu.VMEM((gb, nchunk, w), F32),
                        pltpu.VMEM((gb, nchunk, 2 * S5_STATE), F32)],
        compiler_params=_cparams(("parallel",)),
        name="s5_core",
    )(xg, m, ws, wo, sc, dt)


def _s5_tables(a_re, a_im, log_step, b_re, b_im, c_re, c_im, d_skip):
    ng, ns = a_re.shape
    t = S5_T
    dt = jnp.exp(log_step)[:, None, None]
    lr, li = a_re[:, None, :], a_im[:, None, :]

    def powers(e):
        ee = e.astype(F32)[None, :, None]
        mag = jnp.exp(lr * dt * ee)
        ang = li * dt * ee
        return mag * jnp.cos(ang), mag * jnp.sin(ang)

    lb_re, lb_im = powers(jnp.arange(1, 2))
    n_re, n_im = lb_re - 1.0, lb_im
    den = lr * lr + li * li
    q_re = ((n_re * lr + n_im * li) / den)[:, 0, :, None]
    q_im = ((n_im * lr - n_re * li) / den)[:, 0, :, None]
    bb_re = q_re * b_re - q_im * b_im
    bb_im = q_re * b_im + q_im * b_re

    def c_times(pr, pi):
        cr, ci = c_re[:, None], c_im[:, None]
        pr, pi = pr[:, :, None, :], pi[:, :, None, :]
        return cr * pr - ci * pi, cr * pi + ci * pr

    ca_re, ca_im = c_times(*powers(jnp.arange(0, t)))
    bt_re, bt_im = bb_re.transpose(0, 2, 1)[:, None, :, None], bb_im.transpose(0, 2, 1)[:, None, :, None]
    kern = jnp.sum(ca_re[:, :, None] * bt_re - ca_im[:, :, None] * bt_im, axis=-1)
    lag = jnp.arange(t)[None, :] - jnp.arange(t)[:, None]
    m = jnp.where((lag >= 0)[None, :, :, None, None], kern[:, jnp.clip(lag, 0, t - 1)], 0.0)
    m = m.transpose(0, 1, 3, 2, 4).reshape(ng, t * S5_GROUP, t * S5_GROUP)

    pr, pi = powers(jnp.arange(t - 1, -1, -1))
    pr, pi = pr[:, :, None, :], pi[:, :, None, :]
    br, bi = bb_re.transpose(0, 2, 1)[:, None], bb_im.transpose(0, 2, 1)[:, None]
    ws_re, ws_im = pr * br - pi * bi, pr * bi + pi * br
    ws = jnp.concatenate([ws_re, ws_im, ws_im, ws_re], axis=-1).reshape(ng, t * S5_GROUP, 4 * ns)

    co_re, co_im = c_times(*powers(jnp.arange(1, t + 1)))
    wo = jnp.concatenate([co_re.transpose(0, 3, 1, 2), -co_im.transpose(0, 3, 1, 2)], axis=1)
    wo = wo.reshape(ng, 2 * ns, t * S5_GROUP)

    def pair(e, rows_from):
        er, ei = powers(e)
        keep = (jnp.arange(SUBLANES_V7X) >= rows_from)[None, :, None]
        a1 = jnp.where(keep, jnp.concatenate([er, er], axis=-1), 0.0)
        a2 = jnp.where(keep, jnp.concatenate([-ei, ei], axis=-1), 0.0)
        return [a1, a2]

    sc = []
    for q in (1, 2, 4):
        sc += pair(jnp.full((SUBLANES_V7X,), t * q), q)
    sc += pair(t * jnp.arange(1, SUBLANES_V7X + 1), 0)
    sc = jnp.concatenate(sc, axis=1)

    d_t = jnp.tile(d_skip.reshape(ng, 1, S5_GROUP), (1, 1, t))

    r = jnp.arange(t * S5_GROUP)
    half, slot, chan = r // LANES_V7X, (r % LANES_V7X) // S5_GROUP, r % S5_GROUP
    g8 = jnp.arange(ng)[:, None] % GROUPS_PER_TILE
    step = SUBLANES_V7X * half[None] + (g8 - slot[None]) % SUBLANES_V7X
    idx = step * S5_GROUP + chan[None]
    perm = (idx[:, None, :] == r[None, :, None]).astype(BF16)
    m, ws, wo = m.astype(BF16), ws.astype(BF16), wo.astype(BF16)
    m = jnp.einsum("gab,gbq->gaq", m, perm, preferred_element_type=F32).astype(BF16)
    m = jnp.einsum("gap,gaq->gpq", perm, m, preferred_element_type=F32).astype(BF16)
    ws = jnp.einsum("gap,gan->gpn", perm, ws, preferred_element_type=F32).astype(BF16)
    wo = jnp.einsum("gna,gaq->gnq", wo, perm, preferred_element_type=F32).astype(BF16)
    return m, ws, wo, sc, d_t


def _glu_kernel(yg_ref, wa_ref, wb_ref, x_ref, o_ref, z_ref, lhs_ref):
    jb, pitch = REGROUP_CHUNKS, REGROUP_PITCH
    ntile = lhs_ref.shape[1] // LANES_V7X

    @pl.when(pl.program_id(1) == 0)
    def _():
        for g in range(yg_ref.shape[0]):
            t, g8 = divmod(g, GROUPS_PER_TILE)
            y = yg_ref[g].astype(F32)
            rows = slice(g8 * pitch, g8 * pitch + jb)
            z_ref[2 * t, rows, :] = y[:, :LANES_V7X]
            z_ref[2 * t + 1, rows, :] = y[:, LANES_V7X:]
        bits = _slot_bits()

        def body(j, _):
            r0 = pl.multiple_of(j * S5_T, S5_T)
            for t in range(ntile):
                halves = [_group_to_time_vreg(z_ref[2 * t + half, pl.ds(j, SUBLANES_V7X, stride=pitch), :], bits)
                          for half in range(2)]
                lhs_ref[pl.ds(r0, S5_T), t * LANES_V7X:(t + 1) * LANES_V7X] = (
                    jnp.concatenate(halves, axis=0).astype(lhs_ref.dtype))
            return 0

        lax.fori_loop(0, jb, body, 0, unroll=REGROUP_UNROLL)

    g = lhs_ref[...]
    a = jnp.dot(g, wa_ref[...], preferred_element_type=F32)
    b = jnp.dot(g, wb_ref[...], preferred_element_type=F32)
    o_ref[...] = x_ref[...] + a / (1.0 + jnp.exp(-b))


def _glu_residual(yg, w, layer, x, tn=512):
    ng, nchunk, width = yg.shape
    n, d = x.shape
    k = ng * S5_GROUP
    tm = REGROUP_CHUNKS * S5_T
    nj = d // tn
    return pl.pallas_call(
        _glu_kernel,
        grid=(n // tm, nj),
        in_specs=[pl.BlockSpec((ng, REGROUP_CHUNKS, width), lambda i, j: (0, i, 0)),
                  pl.BlockSpec((None, k, tn), lambda i, j: (layer, 0, j)),
                  pl.BlockSpec((None, k, tn), lambda i, j: (layer, 0, j + nj)),
                  pl.BlockSpec((tm, tn), lambda i, j: (i, j))],
        out_specs=pl.BlockSpec((tm, tn), lambda i, j: (i, j)),
        out_shape=jax.ShapeDtypeStruct((n, d), F32),
        scratch_shapes=[pltpu.VMEM((2 * k // LANES_V7X, GROUPS_PER_TILE * REGROUP_PITCH, LANES_V7X), F32),
                        pltpu.VMEM((tm, k), BF16)],
        compiler_params=_cparams(("parallel", "arbitrary")),
        name="glu_residual",
    )(yg, w, w, x)


def _linres_kernel(a_ref, w_ref, r_ref, o_ref):
    o_ref[...] = r_ref[...] + jnp.dot(a_ref[...], w_ref[...], preferred_element_type=F32)


def _linear_residual(a, w, layer, r, tm=1024, tn=2048):
    n, k = a.shape
    d = w.shape[2]
    return pl.pallas_call(
        _linres_kernel,
        grid=(n // tm, d // tn),
        in_specs=[pl.BlockSpec((tm, k), lambda i, j: (i, 0)),
                  pl.BlockSpec((None, k, tn), lambda i, j: (layer, 0, j)),
                  pl.BlockSpec((tm, tn), lambda i, j: (i, j))],
        out_specs=pl.BlockSpec((tm, tn), lambda i, j: (i, j)),
        out_shape=jax.ShapeDtypeStruct((n, d), F32),
        compiler_params=_cparams(("parallel", "arbitrary")),
        name="linear_residual",
    )(a, w, r)


def _ffn_kernel(h_ref, gn_ref, wg_ref, wu_ref, wd_ref, gf_ref, o_ref, hn_ref, *, final_norm):
    j = pl.program_id(1)

    @pl.when(j == 0)
    def _():
        h = h_ref[...]
        hn_ref[...] = _rms(h, gn_ref[...]).astype(hn_ref.dtype)
        o_ref[...] = h

    hn = hn_ref[...]
    gate = jnp.dot(hn, wg_ref[...], preferred_element_type=F32)
    up = jnp.dot(hn, wu_ref[...], preferred_element_type=F32)
    act = (gate / (1.0 + jnp.exp(-gate)) * up).astype(BF16)
    o_ref[...] += jnp.dot(act, wd_ref[...], preferred_element_type=F32)

    if final_norm:
        @pl.when(j == pl.num_programs(1) - 1)
        def _():
            o_ref[...] = _rms(o_ref[...], gf_ref[...])


def _ffn(h, gain, wg, wu, wd, layer, gain_final, final_norm, tm=1024, tf=512):
    n, d = h.shape
    f = wg.shape[2]
    return pl.pallas_call(
        functools.partial(_ffn_kernel, final_norm=final_norm),
        grid=(n // tm, f // tf),
        in_specs=[pl.BlockSpec((tm, d), lambda i, j: (i, 0)),
                  pl.BlockSpec((1, d), lambda i, j: (0, 0)),
                  pl.BlockSpec((None, d, tf), lambda i, j: (layer, 0, j)),
                  pl.BlockSpec((None, d, tf), lambda i, j: (layer, 0, j)),
                  pl.BlockSpec((None, tf, d), lambda i, j: (layer, j, 0)),
                  pl.BlockSpec((1, d), lambda i, j: (0, 0))],
        out_specs=pl.BlockSpec((tm, d), lambda i, j: (i, 0)),
        out_shape=jax.ShapeDtypeStruct((n, d), F32),
        scratch_shapes=[pltpu.VMEM((tm, d), BF16)],
        compiler_params=_cparams(("parallel", "arbitrary")),
        name="ffn_final" if final_norm else "ffn",
    )(h, gain.reshape(1, d), wg, wu, wd, gain_final.reshape(1, d))


def _qkv_kernel(h_ref, gn_ref, w_ref, b_ref, o_ref, hn_ref):
    @pl.when(pl.program_id(1) == 0)
    def _():
        hn_ref[...] = _rms(h_ref[...], gn_ref[...]).astype(hn_ref.dtype)

    acc = jnp.dot(hn_ref[...], w_ref[...], preferred_element_type=F32)
    o_ref[...] = (acc + b_ref[...]).astype(o_ref.dtype)


def _qkv(h, gain, w, layer, b, tm=1024, tn=2560):
    n, d = h.shape
    m = w.shape[2]
    return pl.pallas_call(
        _qkv_kernel,
        grid=(n // tm, m // tn),
        in_specs=[pl.BlockSpec((tm, d), lambda i, j: (i, 0)),
                  pl.BlockSpec((1, d), lambda i, j: (0, 0)),
                  pl.BlockSpec((None, d, tn), lambda i, j: (layer, 0, j)),
                  pl.BlockSpec((1, tn), lambda i, j: (0, j))],
        out_specs=pl.BlockSpec((tm, tn), lambda i, j: (i, j)),
        out_shape=jax.ShapeDtypeStruct((n, m), BF16),
        scratch_shapes=[pltpu.VMEM((tm, d), BF16)],
        compiler_params=_cparams(("parallel", "arbitrary")),
        name="qkv",
    )(h, gain.reshape(1, d), w, b.reshape(1, m))


def _attn_kernel(slope_ref, sink_ref, q_ref, kv_ref, kvp_ref, o_ref, kbuf_ref, *, tq, kv_heads):
    blk = ATTN_BLOCK
    kvw = kv_heads * HEAD_DIM
    step = pl.program_id(0)
    kbuf_ref[0:blk, :] = kvp_ref[...]
    kbuf_ref[blk:, :] = kv_ref[...]

    qi = lax.broadcasted_iota(jnp.int32, (blk, 2 * blk), 0)
    kj = lax.broadcasted_iota(jnp.int32, (blk, 2 * blk), 1)
    dist = qi + blk - kj
    neg_dist = jnp.where((dist >= 0) & (dist < blk), -dist.astype(F32), -jnp.inf)

    def body(b, _):
        r0 = pl.multiple_of(b * blk, blk)
        first = jnp.logical_and(step == 0, b == 0)
        nd = jnp.where(kj < jnp.where(first, blk, 0), -jnp.inf, neg_dist)
        for kvh in range(kv_heads):
            k = kbuf_ref[pl.ds(r0, 2 * blk), kvh * HEAD_DIM:(kvh + 1) * HEAD_DIM]
            v = kbuf_ref[pl.ds(r0, 2 * blk), kvw + kvh * HEAD_DIM:kvw + (kvh + 1) * HEAD_DIM]
            heads = [kvh * Q_PER_KV + g for g in range(Q_PER_KV)]
            q8 = jnp.concatenate(
                [q_ref[pl.ds(r0, blk), h * HEAD_DIM:(h + 1) * HEAD_DIM] for h in heads], axis=0)
            s8 = lax.dot_general(q8, k, (((1,), (1,)), ((), ())), preferred_element_type=F32)
            ps, dens = [], []
            for g, h in enumerate(heads):
                s = s8[g * blk:(g + 1) * blk] * (HEAD_DIM ** -0.5) + slope_ref[h] * nd
                sink = sink_ref[h]
                mx = jnp.maximum(jnp.max(s, axis=-1, keepdims=True), sink)
                p = jnp.exp(s - mx)
                dens.append(jnp.sum(p, axis=-1, keepdims=True) + jnp.exp(sink - mx))
                ps.append(p.astype(BF16))
            o8 = jnp.dot(jnp.concatenate(ps, axis=0), v, preferred_element_type=F32)
            outs = [o8[g * blk:(g + 1) * blk] / dens[g] for g in range(Q_PER_KV)]
            lo = kvh * Q_PER_KV * HEAD_DIM
            o_ref[pl.ds(r0, blk), lo:lo + Q_PER_KV * HEAD_DIM] = (
                jnp.concatenate(outs, axis=-1).astype(o_ref.dtype))
        return 0

    lax.fori_loop(0, tq // blk, body, 0)


def _attention(qkv, slopes, sinks, q_dim, kv_heads, tq=512):
    n = qkv.shape[0]
    kvw = 2 * kv_heads * HEAD_DIM
    kv_col = q_dim // kvw
    per = tq // ATTN_BLOCK
    return pl.pallas_call(
        functools.partial(_attn_kernel, tq=tq, kv_heads=kv_heads),
        grid=(n // tq,),
        in_specs=[pl.BlockSpec(memory_space=pltpu.SMEM),
                  pl.BlockSpec(memory_space=pltpu.SMEM),
                  pl.BlockSpec((tq, q_dim), lambda i: (i, 0)),
                  pl.BlockSpec((tq, kvw), lambda i: (i, kv_col)),
                  pl.BlockSpec((ATTN_BLOCK, kvw), lambda i: (jnp.maximum(i * per - 1, 0), kv_col))],
        out_specs=pl.BlockSpec((tq, q_dim), lambda i: (i, 0)),
        out_shape=jax.ShapeDtypeStruct((n, q_dim), BF16),
        scratch_shapes=[pltpu.VMEM((tq + ATTN_BLOCK, kvw), BF16)],
        compiler_params=_cparams(("parallel",)),
        name="swa_attention",
    )(slopes, sinks, qkv, qkv, qkv)


def kernel(x, norm_mix, s5_a_re, s5_a_im, s5_log_step, s5_b_re, s5_b_im, s5_c_re, s5_c_im, s5_d, s5_w_glu, attn_w_qkv, attn_b_qkv, attn_sinks, attn_w_o, norm_ffn, ffn_w_gate, ffn_w_up, ffn_w_down, norm_final):
    bsz, seq, dm = x.shape
    ng = dm // S5_GROUP
    nchunk = seq // S5_T
    n_heads = attn_sinks.shape[1]
    q_dim = n_heads * HEAD_DIM
    kv_heads = n_heads // Q_PER_KV
    assert bsz == 1 and norm_mix.shape[0] == 2 and seq % 1024 == 0
    assert S5_T * S5_GROUP == 2 * LANES_V7X and s5_a_re.shape[2] == S5_STATE
    assert q_dim % (2 * kv_heads * HEAD_DIM) == 0

    h0 = x.reshape(seq, dm)

    w_gate, w_up, w_down = (w.astype(BF16) for w in (ffn_w_gate, ffn_w_up, ffn_w_down))
    xg = _norm_regroup(h0, norm_mix[0])
    tables = _s5_tables(s5_a_re[0], s5_a_im[0], s5_log_step[0], s5_b_re[0], s5_b_im[0],
                        s5_c_re[0], s5_c_im[0], s5_d[0])
    yg = _s5_core(xg, *tables)
    h1 = _glu_residual(yg, s5_w_glu.astype(BF16), 0, h0)
    h2 = _ffn(h1, norm_ffn[0], w_gate, w_up, w_down, 0, norm_final, final_norm=False)

    qkv = _qkv(h2, norm_mix[1], attn_w_qkv.astype(BF16), 0, attn_b_qkv[0])
    slopes = jnp.exp2(-8.0 * jnp.arange(1, n_heads + 1, dtype=F32) / n_heads)
    o = _attention(qkv, slopes, attn_sinks[0], q_dim, kv_heads)
    h3 = _linear_residual(o, attn_w_o.astype(BF16), 0, h2)
    out = _ffn(h3, norm_ffn[1], w_gate, w_up, w_down, 1, norm_final, final_norm=True)
    return out.reshape(bsz, seq, dm)
```

```python
import functools
import math

import jax
import jax.numpy as jnp
from jax import lax
from jax.experimental import pallas as pl
from jax.experimental.pallas import tpu as pltpu

F32 = jnp.float32
BF16 = jnp.bfloat16

NORM_EPS = 1e-6
S5_GROUP = 16
S5_STATE = 64
S5_T = 16
HEAD_DIM = 64
Q_PER_KV = 8
ATTN_BLOCK = 128
LANES_V7X = 128
SUBLANES_V7X = 8
VMEM_LIMIT_V7X = 56 * 1024 * 1024


def _cparams(sem):
    return pltpu.CompilerParams(dimension_semantics=sem, vmem_limit_bytes=VMEM_LIMIT_V7X)


def _rms(h, gain):
    return h * lax.rsqrt(jnp.mean(h * h, axis=-1, keepdims=True) + NORM_EPS) * gain


GROUPS_PER_TILE = LANES_V7X // S5_GROUP
REGROUP_CHUNKS = 64
REGROUP_PITCH = REGROUP_CHUNKS + 8
REGROUP_UNROLL = 2


def _slot_bits():
    shape = (SUBLANES_V7X, LANES_V7X)
    sub = lax.broadcasted_iota(jnp.int32, shape, 0)
    lane = lax.broadcasted_iota(jnp.int32, shape, 1)
    return ([((sub >> b) & 1) == 1 for b in range(3)],
            [((lane >> (4 + b)) & 1) == 1 for b in range(3)])


def _time_to_group_vreg(x, bits):
    sub_bits, slot_bits = bits
    for b in range(3):
        x = jnp.where(sub_bits[b], pltpu.roll(x, LANES_V7X - (S5_GROUP << b), 1), x)
    for b in range(3):
        x = jnp.where(slot_bits[b], pltpu.roll(x, 1 << b, 0), x)
    return x


def _group_to_time_vreg(z, bits):
    sub_bits, slot_bits = bits
    for b in range(3):
        z = jnp.where(slot_bits[b], pltpu.roll(z, SUBLANES_V7X - (1 << b), 0), z)
    for b in range(3):
        z = jnp.where(sub_bits[b], pltpu.roll(z, S5_GROUP << b, 1), z)
    return z


def _norm_regroup_kernel(h_ref, g_ref, o_ref, inv_ref, z_ref):
    jb, pitch = REGROUP_CHUNKS, REGROUP_PITCH
    ntile = h_ref.shape[1] // LANES_V7X
    h = h_ref[...]
    inv = lax.rsqrt(jnp.mean(h * h, axis=-1, keepdims=True) + NORM_EPS)
    inv_ref[...] = jnp.broadcast_to(inv, inv_ref.shape)
    bits = _slot_bits()

    def body(j, _):
        for half in range(2):
            r0 = pl.multiple_of(j * S5_T + half * SUBLANES_V7X, SUBLANES_V7X)
            scale = inv_ref[pl.ds(r0, SUBLANES_V7X), :]
            for t in range(ntile):
                lanes = slice(t * LANES_V7X, (t + 1) * LANES_V7X)
                x = h_ref[pl.ds(r0, SUBLANES_V7X), lanes] * scale * g_ref[:, lanes]
                z_ref[2 * t + half, pl.ds(j, SUBLANES_V7X, stride=pitch), :] = _time_to_group_vreg(x, bits)
        return 0

    lax.fori_loop(0, jb, body, 0, unroll=REGROUP_UNROLL)
    for g in range(o_ref.shape[0]):
        t, g8 = divmod(g, GROUPS_PER_TILE)
        rows = slice(g8 * pitch, g8 * pitch + jb)
        o_ref[g] = jnp.concatenate([z_ref[2 * t, rows, :], z_ref[2 * t + 1, rows, :]],
                                   axis=-1).astype(o_ref.dtype)


def _norm_regroup(h, gain):
    n, d = h.shape
    tm = REGROUP_CHUNKS * S5_T
    ng = d // S5_GROUP
    return pl.pallas_call(
        _norm_regroup_kernel,
        grid=(n // tm,),
        in_specs=[pl.BlockSpec((tm, d), lambda i: (i, 0)),
                  pl.BlockSpec((1, d), lambda i: (0, 0))],
        out_specs=pl.BlockSpec((ng, REGROUP_CHUNKS, S5_T * S5_GROUP), lambda i: (0, i, 0)),
        out_shape=jax.ShapeDtypeStruct((ng, n // S5_T, S5_T * S5_GROUP), BF16),
        scratch_shapes=[pltpu.VMEM((tm, LANES_V7X), F32),
                        pltpu.VMEM((2 * d // LANES_V7X, GROUPS_PER_TILE * REGROUP_PITCH, LANES_V7X), F32)],
        compiler_params=_cparams(("parallel",)),
        name="norm_regroup",
    )(h, gain.reshape(1, d))


def _s5_kernel(x_ref, m_ref, ws_ref, wo_ref, sc_ref, d_ref, o_ref, s_ref, h_ref, *, gb, nchunk):
    for g in range(gb):
        s_ref[g] = jnp.dot(x_ref[g], ws_ref[g], preferred_element_type=F32)

    row = lax.broadcasted_iota(jnp.int32, (SUBLANES_V7X, LANES_V7X), 0)
    zero = jnp.zeros((SUBLANES_V7X, LANES_V7X), F32)

    def body(i, carry):
        r0 = pl.multiple_of(i * SUBLANES_V7X, SUBLANES_V7X)
        out = []
        for g in range(gb):
            cb, cbs = carry[g]
            z = s_ref[g, pl.ds(r0, SUBLANES_V7X), 0:LANES_V7X]
            zs = s_ref[g, pl.ds(r0, SUBLANES_V7X), LANES_V7X:2 * LANES_V7X]
            for qi, q in enumerate((1, 2, 4)):
                a1 = sc_ref[g, 16 * qi:16 * qi + 8, :]
                a2 = sc_ref[g, 16 * qi + 8:16 * qi + 16, :]
                zr = pltpu.roll(z, q, axis=0)
                zsr = pltpu.roll(zs, q, axis=0)
                z, zs = z + a1 * zr + a2 * zsr, zs + a1 * zsr - a2 * zr
            p1 = sc_ref[g, 48:56, :]
            p2 = sc_ref[g, 56:64, :]
            hinc = z + p1 * cb + p2 * cbs
            hincs = zs + p1 * cbs - p2 * cb
            h_ref[g, pl.ds(r0, SUBLANES_V7X), :] = jnp.where(row == 0, cb, pltpu.roll(hinc, 1, axis=0))
            out.append((jnp.broadcast_to(hinc[7:8, :], (SUBLANES_V7X, LANES_V7X)),
                        jnp.broadcast_to(hincs[7:8, :], (SUBLANES_V7X, LANES_V7X))))
        return tuple(out)

    lax.fori_loop(0, nchunk // SUBLANES_V7X, body, tuple((zero, zero) for _ in range(gb)))

    for g in range(gb):
        x = x_ref[g]
        y = jnp.dot(x, m_ref[g], preferred_element_type=F32)
        y = y + jnp.dot(h_ref[g].astype(BF16), wo_ref[g], preferred_element_type=F32)
        y = y + d_ref[g] * x.astype(F32)
        o_ref[g] = jax.nn.gelu(y).astype(o_ref.dtype)


def _s5_core(xg, m, ws, wo, sc, dt, gb=8):
    ng, nchunk, w = xg.shape
    spec3 = lambda a, b: pl.BlockSpec((gb, a, b), lambda i: (i, 0, 0))
    return pl.pallas_call(
        functools.partial(_s5_kernel, gb=gb, nchunk=nchunk),
        grid=(ng // gb,),
        in_specs=[spec3(nchunk, w), spec3(w, w), spec3(w, w), spec3(2 * S5_STATE, w),
                  spec3(64, LANES_V7X), spec3(1, w)],
        out_specs=spec3(nchunk, w),
        out_shape=jax.ShapeDtypeStruct((ng, nchunk, w), BF16),
        scratch_shapes=[pltpu.VMEM((gb, nchunk, w), F32),
                        pltpu.VMEM((gb, nchunk, 2 * S5_STATE), F32)],
        compiler_params=_cparams(("parallel",)),
        name="s5_core",
    )(xg, m, ws, wo, sc, dt)


def _s5_tables(a_re, a_im, log_step, b_re, b_im, c_re, c_im, d_skip):
    ng, ns = a_re.shape
    t = S5_T
    dt = jnp.exp(log_step)[:, None, None]
    lr, li = a_re[:, None, :], a_im[:, None, :]

    def powers(e):
        ee = e.astype(F32)[None, :, None]
        mag = jnp.exp(lr * dt * ee)
        ang = li * dt * ee
        return mag * jnp.cos(ang), mag * jnp.sin(ang)

    lb_re, lb_im = powers(jnp.arange(1, 2))
    n_re, n_im = lb_re - 1.0, lb_im
    den = lr * lr + li * li
    q_re = ((n_re * lr + n_im * li) / den)[:, 0, :, None]
    q_im = ((n_im * lr - n_re * li) / den)[:, 0, :, None]
    bb_re = q_re * b_re - q_im * b_im
    bb_im = q_re * b_im + q_im * b_re

    def c_times(pr, pi):
        cr, ci = c_re[:, None], c_im[:, None]
        pr, pi = pr[:, :, None, :], pi[:, :, None, :]
        return cr * pr - ci * pi, cr * pi + ci * pr

    pr0, pi0 = powers(jnp.arange(0, t))
    pr0, pi0 = pr0.transpose(0, 2, 1)[..., None], pi0.transpose(0, 2, 1)[..., None]
    ct_re, ct_im = c_re.transpose(0, 2, 1)[:, :, None, :], c_im.transpose(0, 2, 1)[:, :, None, :]
    ca_re = (ct_re * pr0 - ct_im * pi0).reshape(ng, 1, ns, t * S5_GROUP)
    ca_im = (ct_re * pi0 + ct_im * pr0).reshape(ng, 1, ns, t * S5_GROUP)
    bt_re, bt_im = bb_re.transpose(0, 2, 1)[..., None], bb_im.transpose(0, 2, 1)[..., None]
    kern = jnp.sum(ca_re * bt_re - ca_im * bt_im, axis=2).reshape(ng, S5_GROUP, t, S5_GROUP)
    lag = jnp.arange(t)[None, :] - jnp.arange(t)[:, None]
    m = jnp.where((lag >= 0)[None, None, :, :, None], kern[:, :, jnp.clip(lag, 0, t - 1)], 0.0)
    m = m.transpose(0, 2, 1, 3, 4).reshape(ng, t * S5_GROUP, t * S5_GROUP)

    pr, pi = powers(jnp.arange(t - 1, -1, -1))
    pr, pi = pr[:, :, None, :], pi[:, :, None, :]
    br, bi = bb_re.transpose(0, 2, 1)[:, None], bb_im.transpose(0, 2, 1)[:, None]
    ws_re, ws_im = pr * br - pi * bi, pr * bi + pi * br
    ws = jnp.concatenate([ws_re, ws_im, ws_im, ws_re], axis=-1).reshape(ng, t * S5_GROUP, 4 * ns)

    co_re, co_im = c_times(*powers(jnp.arange(1, t + 1)))
    wo = jnp.concatenate([co_re.transpose(0, 3, 1, 2), -co_im.transpose(0, 3, 1, 2)], axis=1)
    wo = wo.reshape(ng, 2 * ns, t * S5_GROUP)

    def pair(e, rows_from):
        er, ei = powers(e)
        keep = (jnp.arange(SUBLANES_V7X) >= rows_from)[None, :, None]
        a1 = jnp.where(keep, jnp.concatenate([er, er], axis=-1), 0.0)
        a2 = jnp.where(keep, jnp.concatenate([-ei, ei], axis=-1), 0.0)
        return [a1, a2]

    sc = []
    for q in (1, 2, 4):
        sc += pair(jnp.full((SUBLANES_V7X,), t * q), q)
    sc += pair(t * jnp.arange(1, SUBLANES_V7X + 1), 0)
    sc = jnp.concatenate(sc, axis=1)

    d_t = jnp.tile(d_skip.reshape(ng, 1, S5_GROUP), (1, 1, t))

    r = jnp.arange(t * S5_GROUP)
    half, slot, chan = r // LANES_V7X, (r % LANES_V7X) // S5_GROUP, r % S5_GROUP
    g8 = jnp.arange(ng)[:, None] % GROUPS_PER_TILE
    step = SUBLANES_V7X * half[None] + (g8 - slot[None]) % SUBLANES_V7X
    idx = step * S5_GROUP + chan[None]
    perm = (idx[:, None, :] == r[None, :, None]).astype(BF16)
    m, ws, wo = m.astype(BF16), ws.astype(BF16), wo.astype(BF16)
    m = jnp.einsum("gab,gbq->gaq", m, perm, preferred_element_type=F32).astype(BF16)
    m = jnp.einsum("gap,gaq->gpq", perm, m, preferred_element_type=F32).astype(BF16)
    ws = jnp.einsum("gap,gan->gpn", perm, ws, preferred_element_type=F32).astype(BF16)
    wo = jnp.einsum("gna,gaq->gnq", wo, perm, preferred_element_type=F32).astype(BF16)
    return m, ws, wo, sc, d_t


def _glu_kernel(yg_ref, wa_ref, wb_ref, x_ref, o_ref, z_ref, lhs_ref):
    jb, pitch = REGROUP_CHUNKS, REGROUP_PITCH
    ntile = lhs_ref.shape[1] // LANES_V7X

    @pl.when(pl.program_id(1) == 0)
    def _():
        for g in range(yg_ref.shape[0]):
            t, g8 = divmod(g, GROUPS_PER_TILE)
            y = yg_ref[g].astype(F32)
            rows = slice(g8 * pitch, g8 * pitch + jb)
            z_ref[2 * t, rows, :] = y[:, :LANES_V7X]
            z_ref[2 * t + 1, rows, :] = y[:, LANES_V7X:]
        bits = _slot_bits()

        def body(j, _):
            r0 = pl.multiple_of(j * S5_T, S5_T)
            for t in range(ntile):
                halves = [_group_to_time_vreg(z_ref[2 * t + half, pl.ds(j, SUBLANES_V7X, stride=pitch), :], bits)
                          for half in range(2)]
                lhs_ref[pl.ds(r0, S5_T), t * LANES_V7X:(t + 1) * LANES_V7X] = (
                    jnp.concatenate(halves, axis=0).astype(lhs_ref.dtype))
            return 0

        lax.fori_loop(0, jb, body, 0, unroll=REGROUP_UNROLL)

    g = lhs_ref[...]
    a = jnp.dot(g, wa_ref[...], preferred_element_type=F32)
    b = jnp.dot(g, wb_ref[...], preferred_element_type=F32)
    o_ref[...] = x_ref[...] + a / (1.0 + jnp.exp(-b))


def _glu_residual(yg, w, layer, x, tn=512):
    ng, nchunk, width = yg.shape
    n, d = x.shape
    k = ng * S5_GROUP
    tm = REGROUP_CHUNKS * S5_T
    nj = d // tn
    return pl.pallas_call(
        _glu_kernel,
        grid=(n // tm, nj),
        in_specs=[pl.BlockSpec((ng, REGROUP_CHUNKS, width), lambda i, j: (0, i, 0)),
                  pl.BlockSpec((None, k, tn), lambda i, j: (layer, 0, j)),
                  pl.BlockSpec((None, k, tn), lambda i, j: (layer, 0, j + nj)),
                  pl.BlockSpec((tm, tn), lambda i, j: (i, j))],
        out_specs=pl.BlockSpec((tm, tn), lambda i, j: (i, j)),
        out_shape=jax.ShapeDtypeStruct((n, d), F32),
        scratch_shapes=[pltpu.VMEM((2 * k // LANES_V7X, GROUPS_PER_TILE * REGROUP_PITCH, LANES_V7X), F32),
                        pltpu.VMEM((tm, k), BF16)],
        compiler_params=_cparams(("parallel", "arbitrary")),
        name="glu_residual",
    )(yg, w, w, x)


def _linres_kernel(a_ref, w_ref, r_ref, o_ref):
    o_ref[...] = r_ref[...] + jnp.dot(a_ref[...], w_ref[...], preferred_element_type=F32)


def _linear_residual(a, w, layer, r, tm=1024, tn=2048):
    n, k = a.shape
    d = w.shape[2]
    return pl.pallas_call(
        _linres_kernel,
        grid=(n // tm, d // tn),
        in_specs=[pl.BlockSpec((tm, k), lambda i, j: (i, 0)),
                  pl.BlockSpec((None, k, tn), lambda i, j: (layer, 0, j)),
                  pl.BlockSpec((tm, tn), lambda i, j: (i, j))],
        out_specs=pl.BlockSpec((tm, tn), lambda i, j: (i, j)),
        out_shape=jax.ShapeDtypeStruct((n, d), F32),
        compiler_params=_cparams(("parallel", "arbitrary")),
        name="linear_residual",
    )(a, w, r)


def _ffn_kernel(h_ref, gn_ref, wg_ref, wu_ref, wd_ref, gf_ref, o_ref, hn_ref, *, final_norm):
    j = pl.program_id(1)

    @pl.when(j == 0)
    def _():
        h = h_ref[...]
        hn_ref[...] = _rms(h, gn_ref[...]).astype(hn_ref.dtype)
        o_ref[...] = h

    hn = hn_ref[...]
    gate = jnp.dot(hn, wg_ref[...], preferred_element_type=F32)
    up = jnp.dot(hn, wu_ref[...], preferred_element_type=F32)
    act = (gate / (1.0 + jnp.exp(-gate)) * up).astype(BF16)
    o_ref[...] += jnp.dot(act, wd_ref[...], preferred_element_type=F32)

    if final_norm:
        @pl.when(j == pl.num_programs(1) - 1)
        def _():
            o_ref[...] = _rms(o_ref[...], gf_ref[...])


def _ffn(h, gain, wg, wu, wd, layer, gain_final, final_norm, tm=1024, tf=512):
    n, d = h.shape
    f = wg.shape[2]
    return pl.pallas_call(
        functools.partial(_ffn_kernel, final_norm=final_norm),
        grid=(n // tm, f // tf),
        in_specs=[pl.BlockSpec((tm, d), lambda i, j: (i, 0)),
                  pl.BlockSpec((1, d), lambda i, j: (0, 0)),
                  pl.BlockSpec((None, d, tf), lambda i, j: (layer, 0, j)),
                  pl.BlockSpec((None, d, tf), lambda i, j: (layer, 0, j)),
                  pl.BlockSpec((None, tf, d), lambda i, j: (layer, j, 0)),
                  pl.BlockSpec((1, d), lambda i, j: (0, 0))],
        out_specs=pl.BlockSpec((tm, d), lambda i, j: (i, 0)),
        out_shape=jax.ShapeDtypeStruct((n, d), F32),
        scratch_shapes=[pltpu.VMEM((tm, d), BF16)],
        compiler_params=_cparams(("parallel", "arbitrary")),
        name="ffn_final" if final_norm else "ffn",
    )(h, gain.reshape(1, d), wg, wu, wd, gain_final.reshape(1, d))


def _qkv_kernel(h_ref, gn_ref, w_ref, b_ref, s_ref, o_ref, hn_ref):
    @pl.when(pl.program_id(1) == 0)
    def _():
        hn_ref[...] = _rms(h_ref[...], gn_ref[...]).astype(hn_ref.dtype)

    acc = jnp.dot(hn_ref[...], w_ref[...], preferred_element_type=F32)
    o_ref[...] = ((acc + b_ref[...]) * s_ref[...]).astype(o_ref.dtype)


def _qkv(h, gain, w, layer, b, q_dim, tm=1024, tn=2560):
    n, d = h.shape
    m = w.shape[2]
    assert math.log2(HEAD_DIM).is_integer() and int(math.log2(HEAD_DIM)) % 2 == 0
    col_scale = jnp.where(jnp.arange(m) < q_dim, HEAD_DIM ** -0.5, 1.0).astype(F32)
    return pl.pallas_call(
        _qkv_kernel,
        grid=(n // tm, m // tn),
        in_specs=[pl.BlockSpec((tm, d), lambda i, j: (i, 0)),
                  pl.BlockSpec((1, d), lambda i, j: (0, 0)),
                  pl.BlockSpec((None, d, tn), lambda i, j: (layer, 0, j)),
                  pl.BlockSpec((1, tn), lambda i, j: (0, j)),
                  pl.BlockSpec((1, tn), lambda i, j: (0, j))],
        out_specs=pl.BlockSpec((tm, tn), lambda i, j: (i, j)),
        out_shape=jax.ShapeDtypeStruct((n, m), BF16),
        scratch_shapes=[pltpu.VMEM((tm, d), BF16)],
        compiler_params=_cparams(("parallel", "arbitrary")),
        name="qkv",
    )(h, gain.reshape(1, d), w, b.reshape(1, m), col_scale.reshape(1, m))


def _attn_kernel(slope_ref, sink_ref, q_ref, kv_ref, kvp_ref, o_ref, kbuf_ref, *, tq, kv_heads):
    blk = ATTN_BLOCK
    kvw = kv_heads * HEAD_DIM
    step = pl.program_id(0)
    kbuf_ref[0:blk, :] = kvp_ref[...]
    kbuf_ref[blk:, :] = kv_ref[...]

    qi = lax.broadcasted_iota(jnp.int32, (blk, 2 * blk), 0)
    kj = lax.broadcasted_iota(jnp.int32, (blk, 2 * blk), 1)
    dist = qi + blk - kj
    neg_dist = jnp.where((dist >= 0) & (dist < blk), -dist.astype(F32), -jnp.inf)

    def body(b, _):
        r0 = pl.multiple_of(b * blk, blk)
        first = jnp.logical_and(step == 0, b == 0)
        nd = jnp.where(kj < jnp.where(first, blk, 0), -jnp.inf, neg_dist)
        for kvh in range(kv_heads):
            k = kbuf_ref[pl.ds(r0, 2 * blk), kvh * HEAD_DIM:(kvh + 1) * HEAD_DIM]
            v = kbuf_ref[pl.ds(r0, 2 * blk), kvw + kvh * HEAD_DIM:kvw + (kvh + 1) * HEAD_DIM]
            heads = [kvh * Q_PER_KV + g for g in range(Q_PER_KV)]
            q8 = jnp.concatenate(
                [q_ref[pl.ds(r0, blk), h * HEAD_DIM:(h + 1) * HEAD_DIM] for h in heads], axis=0)
            s8 = lax.dot_general(q8, k, (((1,), (1,)), ((), ())), preferred_element_type=F32)
            ps, dens = [], []
            for g, h in enumerate(heads):
                s = s8[g * blk:(g + 1) * blk] + slope_ref[h] * nd
                sink = sink_ref[h]
                mx = jnp.maximum(jnp.max(s, axis=-1, keepdims=True), sink)
                p = jnp.exp(s - mx)
                dens.append(jnp.sum(p, axis=-1, keepdims=True) + jnp.exp(sink - mx))
                ps.append(p.astype(BF16))
            o8 = jnp.dot(jnp.concatenate(ps, axis=0), v, preferred_element_type=F32)
            outs = [o8[g * blk:(g + 1) * blk] / dens[g] for g in range(Q_PER_KV)]
            lo = kvh * Q_PER_KV * HEAD_DIM
            o_ref[pl.ds(r0, blk), lo:lo + Q_PER_KV * HEAD_DIM] = (
                jnp.concatenate(outs, axis=-1).astype(o_ref.dtype))
        return 0

    lax.fori_loop(0, tq // blk, body, 0)


def _attention(qkv, slopes, sinks, q_dim, kv_heads, tq=512):
    n = qkv.shape[0]
    kvw = 2 * kv_heads * HEAD_DIM
    kv_col = q_dim // kvw
    per = tq // ATTN_BLOCK
    return pl.pallas_call(
        functools.partial(_attn_kernel, tq=tq, kv_heads=kv_heads),
        grid=(n // tq,),
        in_specs=[pl.BlockSpec(memory_space=pltpu.SMEM),
                  pl.BlockSpec(memory_space=pltpu.SMEM),
                  pl.BlockSpec((tq, q_dim), lambda i: (i, 0)),
                  pl.BlockSpec((tq, kvw), lambda i: (i, kv_col)),
                  pl.BlockSpec((ATTN_BLOCK, kvw), lambda i: (jnp.maximum(i * per - 1, 0), kv_col))],
        out_specs=pl.BlockSpec((tq, q_dim), lambda i: (i, 0)),
        out_shape=jax.ShapeDtypeStruct((n, q_dim), BF16),
        scratch_shapes=[pltpu.VMEM((tq + ATTN_BLOCK, kvw), BF16)],
        compiler_params=_cparams(("parallel",)),
        name="swa_attention",
    )(slopes, sinks, qkv, qkv, qkv)


def kernel(x, norm_mix, s5_a_re, s5_a_im, s5_log_step, s5_b_re, s5_b_im, s5_c_re, s5_c_im, s5_d, s5_w_glu, attn_w_qkv, attn_b_qkv, attn_sinks, attn_w_o, norm_ffn, ffn_w_gate, ffn_w_up, ffn_w_down, norm_final):
    bsz, seq, dm = x.shape
    ng = dm // S5_GROUP
    nchunk = seq // S5_T
    n_heads = attn_sinks.shape[1]
    q_dim = n_heads * HEAD_DIM
    kv_heads = n_heads // Q_PER_KV
    assert bsz == 1 and norm_mix.shape[0] == 2 and seq % 1024 == 0
    assert S5_T * S5_GROUP == 2 * LANES_V7X and s5_a_re.shape[2] == S5_STATE
    assert q_dim % (2 * kv_heads * HEAD_DIM) == 0

    h0 = x.reshape(seq, dm)

    w_gate, w_up, w_down = (w.astype(BF16) for w in (ffn_w_gate, ffn_w_up, ffn_w_down))
    xg = _norm_regroup(h0, norm_mix[0])
    tables = _s5_tables(s5_a_re[0], s5_a_im[0], s5_log_step[0], s5_b_re[0], s5_b_im[0],
                        s5_c_re[0], s5_c_im[0], s5_d[0])
    yg = _s5_core(xg, *tables)
    h1 = _glu_residual(yg, s5_w_glu.astype(BF16), 0, h0)
    h2 = _ffn(h1, norm_ffn[0], w_gate, w_up, w_down, 0, norm_final, final_norm=False)

    qkv = _qkv(h2, norm_mix[1], attn_w_qkv.astype(BF16), 0, attn_b_qkv[0], q_dim)
    slopes = jnp.exp2(-8.0 * jnp.arange(1, n_heads + 1, dtype=F32) / n_heads)
    o = _attention(qkv, slopes, attn_sinks[0], q_dim, kv_heads)
    h3 = _linear_residual(o, attn_w_o.astype(BF16), 0, h2)
    out = _ffn(h3, norm_ffn[1], w_gate, w_up, w_down, 1, norm_final, final_norm=True)
    return out.reshape(bsz, seq, dm)
```

```python
import functools
import math

import jax
import jax.numpy as jnp
from jax import lax
from jax.experimental import pallas as pl
from jax.experimental.pallas import tpu as pltpu

F32 = jnp.float32
BF16 = jnp.bfloat16

NORM_EPS = 1e-6
S5_GROUP = 16
S5_STATE = 64
S5_T = 16
HEAD_DIM = 64
Q_PER_KV = 8
ATTN_BLOCK = 128
LANES_V7X = 128
SUBLANES_V7X = 8
VMEM_LIMIT_V7X = 56 * 1024 * 1024


def _cparams(sem):
    return pltpu.CompilerParams(dimension_semantics=sem, vmem_limit_bytes=VMEM_LIMIT_V7X)


def _rms(h, gain):
    return h * lax.rsqrt(jnp.mean(h * h, axis=-1, keepdims=True) + NORM_EPS) * gain


GROUPS_PER_TILE = LANES_V7X // S5_GROUP
REGROUP_CHUNKS = 64
REGROUP_PITCH = REGROUP_CHUNKS + 8
REGROUP_UNROLL = 2


def _slot_bits():
    shape = (SUBLANES_V7X, LANES_V7X)
    sub = lax.broadcasted_iota(jnp.int32, shape, 0)
    lane = lax.broadcasted_iota(jnp.int32, shape, 1)
    return ([((sub >> b) & 1) == 1 for b in range(3)],
            [((lane >> (4 + b)) & 1) == 1 for b in range(3)])


def _time_to_group_vreg(x, bits):
    sub_bits, slot_bits = bits
    for b in range(3):
        x = jnp.where(sub_bits[b], pltpu.roll(x, LANES_V7X - (S5_GROUP << b), 1), x)
    for b in range(3):
        x = jnp.where(slot_bits[b], pltpu.roll(x, 1 << b, 0), x)
    return x


def _group_to_time_vreg(z, bits):
    sub_bits, slot_bits = bits
    for b in range(3):
        z = jnp.where(slot_bits[b], pltpu.roll(z, SUBLANES_V7X - (1 << b), 0), z)
    for b in range(3):
        z = jnp.where(sub_bits[b], pltpu.roll(z, S5_GROUP << b, 1), z)
    return z


def _norm_regroup_kernel(h_ref, g_ref, o_ref, inv_ref, z_ref):
    jb, pitch = REGROUP_CHUNKS, REGROUP_PITCH
    ntile = h_ref.shape[1] // LANES_V7X
    h = h_ref[...]
    inv = lax.rsqrt(jnp.mean(h * h, axis=-1, keepdims=True) + NORM_EPS)
    inv_ref[...] = jnp.broadcast_to(inv, inv_ref.shape)
    bits = _slot_bits()

    def body(j, _):
        for half in range(2):
            r0 = pl.multiple_of(j * S5_T + half * SUBLANES_V7X, SUBLANES_V7X)
            scale = inv_ref[pl.ds(r0, SUBLANES_V7X), :]
            for t in range(ntile):
                lanes = slice(t * LANES_V7X, (t + 1) * LANES_V7X)
                x = h_ref[pl.ds(r0, SUBLANES_V7X), lanes] * scale * g_ref[:, lanes]
                z_ref[2 * t + half, pl.ds(j, SUBLANES_V7X, stride=pitch), :] = _time_to_group_vreg(x, bits)
        return 0

    lax.fori_loop(0, jb, body, 0, unroll=REGROUP_UNROLL)
    for g in range(o_ref.shape[0]):
        t, g8 = divmod(g, GROUPS_PER_TILE)
        rows = slice(g8 * pitch, g8 * pitch + jb)
        o_ref[g] = jnp.concatenate([z_ref[2 * t, rows, :], z_ref[2 * t + 1, rows, :]],
                                   axis=-1).astype(o_ref.dtype)


def _norm_regroup(h, gain):
    n, d = h.shape
    tm = REGROUP_CHUNKS * S5_T
    ng = d // S5_GROUP
    return pl.pallas_call(
        _norm_regroup_kernel,
        grid=(n // tm,),
        in_specs=[pl.BlockSpec((tm, d), lambda i: (i, 0)),
                  pl.BlockSpec((1, d), lambda i: (0, 0))],
        out_specs=pl.BlockSpec((ng, REGROUP_CHUNKS, S5_T * S5_GROUP), lambda i: (0, i, 0)),
        out_shape=jax.ShapeDtypeStruct((ng, n // S5_T, S5_T * S5_GROUP), BF16),
        scratch_shapes=[pltpu.VMEM((tm, LANES_V7X), F32),
                        pltpu.VMEM((2 * d // LANES_V7X, GROUPS_PER_TILE * REGROUP_PITCH, LANES_V7X), F32)],
        compiler_params=_cparams(("parallel",)),
        name="norm_regroup",
    )(h, gain.reshape(1, d))


def _s5_kernel(x_ref, m_ref, ws_ref, wo_ref, sc_ref, d_ref, o_ref, s_ref, h_ref, *, gb, nchunk):
    for g in range(gb):
        s_ref[g] = jnp.dot(x_ref[g], ws_ref[g], preferred_element_type=F32)

    row = lax.broadcasted_iota(jnp.int32, (SUBLANES_V7X, LANES_V7X), 0)
    zero = jnp.zeros((SUBLANES_V7X, LANES_V7X), F32)

    def body(i, carry):
        r0 = pl.multiple_of(i * SUBLANES_V7X, SUBLANES_V7X)
        out = []
        for g in range(gb):
            cb, cbs = carry[g]
            z = s_ref[g, pl.ds(r0, SUBLANES_V7X), 0:LANES_V7X]
            zs = s_ref[g, pl.ds(r0, SUBLANES_V7X), LANES_V7X:2 * LANES_V7X]
            for qi, q in enumerate((1, 2, 4)):
                a1 = sc_ref[g, 16 * qi:16 * qi + 8, :]
                a2 = sc_ref[g, 16 * qi + 8:16 * qi + 16, :]
                zr = pltpu.roll(z, q, axis=0)
                zsr = pltpu.roll(zs, q, axis=0)
                z, zs = z + a1 * zr + a2 * zsr, zs + a1 * zsr - a2 * zr
            p1 = sc_ref[g, 48:56, :]
            p2 = sc_ref[g, 56:64, :]
            hinc = z + p1 * cb + p2 * cbs
            hincs = zs + p1 * cbs - p2 * cb
            h_ref[g, pl.ds(r0, SUBLANES_V7X), :] = jnp.where(row == 0, cb, pltpu.roll(hinc, 1, axis=0))
            out.append((jnp.broadcast_to(hinc[7:8, :], (SUBLANES_V7X, LANES_V7X)),
                        jnp.broadcast_to(hincs[7:8, :], (SUBLANES_V7X, LANES_V7X))))
        return tuple(out)

    lax.fori_loop(0, nchunk // SUBLANES_V7X, body, tuple((zero, zero) for _ in range(gb)))

    for g in range(gb):
        x = x_ref[g]
        y = jnp.dot(x, m_ref[g], preferred_element_type=F32)
        y = y + jnp.dot(h_ref[g].astype(BF16), wo_ref[g], preferred_element_type=F32)
        y = y + d_ref[g] * x.astype(F32)
        o_ref[g] = jax.nn.gelu(y).astype(o_ref.dtype)


def _s5_core(xg, m, ws, wo, sc, dt, gb=8):
    ng, nchunk, w = xg.shape
    spec3 = lambda a, b: pl.BlockSpec((gb, a, b), lambda i: (i, 0, 0))
    return pl.pallas_call(
        functools.partial(_s5_kernel, gb=gb, nchunk=nchunk),
        grid=(ng // gb,),
        in_specs=[spec3(nchunk, w), spec3(w, w), spec3(w, w), spec3(2 * S5_STATE, w),
                  spec3(64, LANES_V7X), spec3(1, w)],
        out_specs=spec3(nchunk, w),
        out_shape=jax.ShapeDtypeStruct((ng, nchunk, w), BF16),
        scratch_shapes=[pltpu.VMEM((gb, nchunk, w), F32),
                        pltpu.VMEM((gb, nchunk, 2 * S5_STATE), F32)],
        compiler_params=_cparams(("parallel",)),
        name="s5_core",
    )(xg, m, ws, wo, sc, dt)


def _s5_tables(a_re, a_im, log_step, b_re, b_im, c_re, c_im, d_skip):
    ng, ns = a_re.shape
    t = S5_T
    dt = jnp.exp(log_step)[:, None, None]
    lr, li = a_re[:, None, :], a_im[:, None, :]

    def powers(e):
        ee = e.astype(F32)[None, :, None]
        mag = jnp.exp(lr * dt * ee)
        ang = li * dt * ee
        return mag * jnp.cos(ang), mag * jnp.sin(ang)

    lb_re, lb_im = powers(jnp.arange(1, 2))
    n_re, n_im = lb_re - 1.0, lb_im
    den = lr * lr + li * li
    q_re = ((n_re * lr + n_im * li) / den)[:, 0, :, None]
    q_im = ((n_im * lr - n_re * li) / den)[:, 0, :, None]
    bb_re = q_re * b_re - q_im * b_im
    bb_im = q_re * b_im + q_im * b_re

    def c_times(pr, pi):
        cr, ci = c_re[:, None], c_im[:, None]
        pr, pi = pr[:, :, None, :], pi[:, :, None, :]
        return cr * pr - ci * pi, cr * pi + ci * pr

    pr0, pi0 = powers(jnp.arange(0, t))
    pr0, pi0 = pr0.transpose(0, 2, 1)[..., None], pi0.transpose(0, 2, 1)[..., None]
    ct_re, ct_im = c_re.transpose(0, 2, 1)[:, :, None, :], c_im.transpose(0, 2, 1)[:, :, None, :]
    ca_re = (ct_re * pr0 - ct_im * pi0).reshape(ng, 1, ns, t * S5_GROUP)
    ca_im = (ct_re * pi0 + ct_im * pr0).reshape(ng, 1, ns, t * S5_GROUP)
    bt_re, bt_im = bb_re.transpose(0, 2, 1)[..., None], bb_im.transpose(0, 2, 1)[..., None]
    kern = jnp.sum(ca_re * bt_re - ca_im * bt_im, axis=2).reshape(ng, S5_GROUP, t, S5_GROUP)
    lag = jnp.arange(t)[None, :] - jnp.arange(t)[:, None]
    m = jnp.where((lag >= 0)[None, None, :, :, None], kern[:, :, jnp.clip(lag, 0, t - 1)], 0.0)
    m = m.transpose(0, 2, 1, 3, 4).reshape(ng, t * S5_GROUP, t * S5_GROUP)

    pr, pi = powers(jnp.arange(t - 1, -1, -1))
    pr, pi = pr[:, :, None, :], pi[:, :, None, :]
    br, bi = bb_re.transpose(0, 2, 1)[:, None], bb_im.transpose(0, 2, 1)[:, None]
    ws_re, ws_im = pr * br - pi * bi, pr * bi + pi * br
    ws = jnp.concatenate([ws_re, ws_im, ws_im, ws_re], axis=-1).reshape(ng, t * S5_GROUP, 4 * ns)

    co_re, co_im = c_times(*powers(jnp.arange(1, t + 1)))
    wo = jnp.concatenate([co_re.transpose(0, 3, 1, 2), -co_im.transpose(0, 3, 1, 2)], axis=1)
    wo = wo.reshape(ng, 2 * ns, t * S5_GROUP)

    def pair(e, rows_from):
        er, ei = powers(e)
        keep = (jnp.arange(SUBLANES_V7X) >= rows_from)[None, :, None]
        a1 = jnp.where(keep, jnp.concatenate([er, er], axis=-1), 0.0)
        a2 = jnp.where(keep, jnp.concatenate([-ei, ei], axis=-1), 0.0)
        return [a1, a2]

    sc = []
    for q in (1, 2, 4):
        sc += pair(jnp.full((SUBLANES_V7X,), t * q), q)
    sc += pair(t * jnp.arange(1, SUBLANES_V7X + 1), 0)
    sc = jnp.concatenate(sc, axis=1)

    d_t = jnp.tile(d_skip.reshape(ng, 1, S5_GROUP), (1, 1, t))

    r = jnp.arange(t * S5_GROUP)
    half, slot, chan = r // LANES_V7X, (r % LANES_V7X) // S5_GROUP, r % S5_GROUP
    g8 = jnp.arange(ng)[:, None] % GROUPS_PER_TILE
    step = SUBLANES_V7X * half[None] + (g8 - slot[None]) % SUBLANES_V7X
    idx = step * S5_GROUP + chan[None]
    perm = (idx[:, None, :] == r[None, :, None]).astype(BF16)
    m, ws, wo = m.astype(BF16), ws.astype(BF16), wo.astype(BF16)
    m = jnp.einsum("gab,gbq->gaq", m, perm, preferred_element_type=F32).astype(BF16)
    m = jnp.einsum("gap,gaq->gpq", perm, m, preferred_element_type=F32).astype(BF16)
    ws = jnp.einsum("gap,gan->gpn", perm, ws, preferred_element_type=F32).astype(BF16)
    wo = jnp.einsum("gna,gaq->gnq", wo, perm, preferred_element_type=F32).astype(BF16)
    return m, ws, wo, sc, d_t


def _glu_kernel(yg_ref, wa_ref, wb_ref, x_ref, o_ref, z_ref, lhs_ref):
    jb, pitch = REGROUP_CHUNKS, REGROUP_PITCH
    ntile = lhs_ref.shape[1] // LANES_V7X

    @pl.when(pl.program_id(1) == 0)
    def _():
        for g in range(yg_ref.shape[0]):
            t, g8 = divmod(g, GROUPS_PER_TILE)
            y = yg_ref[g].astype(F32)
            rows = slice(g8 * pitch, g8 * pitch + jb)
            z_ref[2 * t, rows, :] = y[:, :LANES_V7X]
            z_ref[2 * t + 1, rows, :] = y[:, LANES_V7X:]
        bits = _slot_bits()

        def body(j, _):
            r0 = pl.multiple_of(j * S5_T, S5_T)
            for t in range(ntile):
                halves = [_group_to_time_vreg(z_ref[2 * t + half, pl.ds(j, SUBLANES_V7X, stride=pitch), :], bits)
                          for half in range(2)]
                lhs_ref[pl.ds(r0, S5_T), t * LANES_V7X:(t + 1) * LANES_V7X] = (
                    jnp.concatenate(halves, axis=0).astype(lhs_ref.dtype))
            return 0

        lax.fori_loop(0, jb, body, 0, unroll=REGROUP_UNROLL)

    g = lhs_ref[...]
    a = jnp.dot(g, wa_ref[...], preferred_element_type=F32)
    b = jnp.dot(g, wb_ref[...], preferred_element_type=F32)
    o_ref[...] = x_ref[...] + a / (1.0 + jnp.exp(-b))


def _glu_residual(yg, w, layer, x, tn=512):
    ng, nchunk, width = yg.shape
    n, d = x.shape
    k = ng * S5_GROUP
    tm = REGROUP_CHUNKS * S5_T
    nj = d // tn
    return pl.pallas_call(
        _glu_kernel,
        grid=(n // tm, nj),
        in_specs=[pl.BlockSpec((ng, REGROUP_CHUNKS, width), lambda i, j: (0, i, 0)),
                  pl.BlockSpec((None, k, tn), lambda i, j: (layer, 0, j)),
                  pl.BlockSpec((None, k, tn), lambda i, j: (layer, 0, j + nj)),
                  pl.BlockSpec((tm, tn), lambda i, j: (i, j))],
        out_specs=pl.BlockSpec((tm, tn), lambda i, j: (i, j)),
        out_shape=jax.ShapeDtypeStruct((n, d), F32),
        scratch_shapes=[pltpu.VMEM((2 * k // LANES_V7X, GROUPS_PER_TILE * REGROUP_PITCH, LANES_V7X), F32),
                        pltpu.VMEM((tm, k), BF16)],
        compiler_params=_cparams(("parallel", "arbitrary")),
        name="glu_residual",
    )(yg, w, w, x)


def _linres_kernel(a_ref, w_ref, r_ref, o_ref):
    o_ref[...] = r_ref[...] + jnp.dot(a_ref[...], w_ref[...], preferred_element_type=F32)


def _linear_residual(a, w, layer, r, tm=1024, tn=2048):
    n, k = a.shape
    d = w.shape[2]
    return pl.pallas_call(
        _linres_kernel,
        grid=(n // tm, d // tn),
        in_specs=[pl.BlockSpec((tm, k), lambda i, j: (i, 0)),
                  pl.BlockSpec((None, k, tn), lambda i, j: (layer, 0, j)),
                  pl.BlockSpec((tm, tn), lambda i, j: (i, j))],
        out_specs=pl.BlockSpec((tm, tn), lambda i, j: (i, j)),
        out_shape=jax.ShapeDtypeStruct((n, d), F32),
        compiler_params=_cparams(("parallel", "arbitrary")),
        name="linear_residual",
    )(a, w, r)


def _ffn_kernel(h_ref, gn_ref, wg_ref, wu_ref, wd_ref, gf_ref, o_ref, hn_ref, *, final_norm):
    j = pl.program_id(1)

    @pl.when(j == 0)
    def _():
        h = h_ref[...]
        hn_ref[...] = _rms(h, gn_ref[...]).astype(hn_ref.dtype)
        o_ref[...] = h

    hn = hn_ref[...]
    gate = jnp.dot(hn, wg_ref[...], preferred_element_type=F32)
    up = jnp.dot(hn, wu_ref[...], preferred_element_type=F32)
    act = (gate / (1.0 + jnp.exp(-gate)) * up).astype(BF16)
    o_ref[...] += jnp.dot(act, wd_ref[...], preferred_element_type=F32)

    if final_norm:
        @pl.when(j == pl.num_programs(1) - 1)
        def _():
            o_ref[...] = _rms(o_ref[...], gf_ref[...])


def _ffn(h, gain, wg, wu, wd, layer, gain_final, final_norm, tm=1024, tf=512):
    n, d = h.shape
    f = wg.shape[2]
    return pl.pallas_call(
        functools.partial(_ffn_kernel, final_norm=final_norm),
        grid=(n // tm, f // tf),
        in_specs=[pl.BlockSpec((tm, d), lambda i, j: (i, 0)),
                  pl.BlockSpec((1, d), lambda i, j: (0, 0)),
                  pl.BlockSpec((None, d, tf), lambda i, j: (layer, 0, j)),
                  pl.BlockSpec((None, d, tf), lambda i, j: (layer, 0, j)),
                  pl.BlockSpec((None, tf, d), lambda i, j: (layer, j, 0)),
                  pl.BlockSpec((1, d), lambda i, j: (0, 0))],
        out_specs=pl.BlockSpec((tm, d), lambda i, j: (i, 0)),
        out_shape=jax.ShapeDtypeStruct((n, d), F32),
        scratch_shapes=[pltpu.VMEM((tm, d), BF16)],
        compiler_params=_cparams(("parallel", "arbitrary")),
        name="ffn_final" if final_norm else "ffn",
    )(h, gain.reshape(1, d), wg, wu, wd, gain_final.reshape(1, d))


def _qkv_kernel(h_ref, gn_ref, w_ref, b_ref, s_ref, o_ref, hn_ref):
    @pl.when(pl.program_id(1) == 0)
    def _():
        hn_ref[...] = _rms(h_ref[...], gn_ref[...]).astype(hn_ref.dtype)

    acc = jnp.dot(hn_ref[...], w_ref[...], preferred_element_type=F32)
    o_ref[...] = ((acc + b_ref[...]) * s_ref[...]).astype(o_ref.dtype)


def _qkv(h, gain, w, layer, b, q_dim, tm=1024, tn=2560):
    n, d = h.shape
    m = w.shape[2]
    assert math.log2(HEAD_DIM).is_integer() and int(math.log2(HEAD_DIM)) % 2 == 0
    col_scale = jnp.where(jnp.arange(m) < q_dim, HEAD_DIM ** -0.5, 1.0).astype(F32)
    return pl.pallas_call(
        _qkv_kernel,
        grid=(n // tm, m // tn),
        in_specs=[pl.BlockSpec((tm, d), lambda i, j: (i, 0)),
                  pl.BlockSpec((1, d), lambda i, j: (0, 0)),
                  pl.BlockSpec((None, d, tn), lambda i, j: (layer, 0, j)),
                  pl.BlockSpec((1, tn), lambda i, j: (0, j)),
                  pl.BlockSpec((1, tn), lambda i, j: (0, j))],
        out_specs=pl.BlockSpec((tm, tn), lambda i, j: (i, j)),
        out_shape=jax.ShapeDtypeStruct((n, m), BF16),
        scratch_shapes=[pltpu.VMEM((tm, d), BF16)],
        compiler_params=_cparams(("parallel", "arbitrary")),
        name="qkv",
    )(h, gain.reshape(1, d), w, b.reshape(1, m), col_scale.reshape(1, m))


def _attn_kernel(slope_ref, sink_ref, q_ref, kv_ref, kvp_ref, o_ref, kbuf_ref, *, tq, kv_heads):
    blk = ATTN_BLOCK
    kvw = kv_heads * HEAD_DIM
    step = pl.program_id(0)
    kbuf_ref[0:blk, :] = kvp_ref[...]
    kbuf_ref[blk:, :] = kv_ref[...]

    kj = lax.broadcasted_iota(jnp.int32, (blk, blk), 0)
    qi = lax.broadcasted_iota(jnp.int32, (blk, blk), 1)
    own = kj <= qi
    neg_dist = -jnp.where(own, qi - kj, qi + blk - kj).astype(F32)

    def body(b, _):
        r0 = pl.multiple_of(b * blk, blk)
        first = jnp.logical_and(step == 0, b == 0)
        nd = jnp.where(jnp.logical_or(own, jnp.logical_not(first)), neg_dist, -jnp.inf)
        for kvh in range(kv_heads):
            k = kbuf_ref[pl.ds(r0, 2 * blk), kvh * HEAD_DIM:(kvh + 1) * HEAD_DIM]
            v = kbuf_ref[pl.ds(r0, 2 * blk), kvw + kvh * HEAD_DIM:kvw + (kvh + 1) * HEAD_DIM]
            heads = [kvh * Q_PER_KV + g for g in range(Q_PER_KV)]
            q8 = jnp.concatenate(
                [q_ref[pl.ds(r0, blk), h * HEAD_DIM:(h + 1) * HEAD_DIM] for h in heads], axis=0)
            st = lax.dot_general(k, q8, (((1,), (1,)), ((), ())), preferred_element_type=F32)
            ps, dens = [], []
            for g, h in enumerate(heads):
                cols = slice(g * blk, (g + 1) * blk)
                s = jnp.where(own, st[blk:, cols], st[:blk, cols]) + slope_ref[h] * nd
                sink = sink_ref[h]
                mx = jnp.maximum(jnp.max(s, axis=0, keepdims=True), sink)
                p = jnp.exp(s - mx)
                dens.append(jnp.sum(p, axis=0, keepdims=True) + jnp.exp(sink - mx))
                ps.append(jnp.concatenate([jnp.where(own, 0.0, p), jnp.where(own, p, 0.0)], axis=0).astype(BF16))
            ot = lax.dot_general(v, jnp.concatenate(ps, axis=1), (((0,), (0,)), ((), ())),
                                 preferred_element_type=F32)
            o8 = (ot / jnp.concatenate(dens, axis=1)).T
            lo = kvh * Q_PER_KV * HEAD_DIM
            o_ref[pl.ds(r0, blk), lo:lo + Q_PER_KV * HEAD_DIM] = jnp.concatenate(
                [o8[g * blk:(g + 1) * blk] for g in range(Q_PER_KV)], axis=-1).astype(o_ref.dtype)
        return 0

    lax.fori_loop(0, tq // blk, body, 0)


def _attention(qkv, slopes, sinks, q_dim, kv_heads, tq=512):
    n = qkv.shape[0]
    kvw = 2 * kv_heads * HEAD_DIM
    kv_col = q_dim // kvw
    per = tq // ATTN_BLOCK
    return pl.pallas_call(
        functools.partial(_attn_kernel, tq=tq, kv_heads=kv_heads),
        grid=(n // tq,),
        in_specs=[pl.BlockSpec(memory_space=pltpu.SMEM),
                  pl.BlockSpec(memory_space=pltpu.SMEM),
                  pl.BlockSpec((tq, q_dim), lambda i: (i, 0)),
                  pl.BlockSpec((tq, kvw), lambda i: (i, kv_col)),
                  pl.BlockSpec((ATTN_BLOCK, kvw), lambda i: (jnp.maximum(i * per - 1, 0), kv_col))],
        out_specs=pl.BlockSpec((tq, q_dim), lambda i: (i, 0)),
        out_shape=jax.ShapeDtypeStruct((n, q_dim), BF16),
        scratch_shapes=[pltpu.VMEM((tq + ATTN_BLOCK, kvw), BF16)],
        compiler_params=_cparams(("parallel",)),
        name="swa_attention",
    )(slopes, sinks, qkv, qkv, qkv)


def kernel(x, norm_mix, s5_a_re, s5_a_im, s5_log_step, s5_b_re, s5_b_im, s5_c_re, s5_c_im, s5_d, s5_w_glu, attn_w_qkv, attn_b_qkv, attn_sinks, attn_w_o, norm_ffn, ffn_w_gate, ffn_w_up, ffn_w_down, norm_final):
    bsz, seq, dm = x.shape
    ng = dm // S5_GROUP
    nchunk = seq // S5_T
    n_heads = attn_sinks.shape[1]
    q_dim = n_heads * HEAD_DIM
    kv_heads = n_heads // Q_PER_KV
    assert bsz == 1 and norm_mix.shape[0] == 2 and seq % 1024 == 0
    assert S5_T * S5_GROUP == 2 * LANES_V7X and s5_a_re.shape[2] == S5_STATE
    assert q_dim % (2 * kv_heads * HEAD_DIM) == 0

    h0 = x.reshape(seq, dm)

    w_gate, w_up, w_down = (w.astype(BF16) for w in (ffn_w_gate, ffn_w_up, ffn_w_down))
    xg = _norm_regroup(h0, norm_mix[0])
    tables = _s5_tables(s5_a_re[0], s5_a_im[0], s5_log_step[0], s5_b_re[0], s5_b_im[0],
                        s5_c_re[0], s5_c_im[0], s5_d[0])
    yg = _s5_core(xg, *tables)
    h1 = _glu_residual(yg, s5_w_glu.astype(BF16), 0, h0)
    h2 = _ffn(h1, norm_ffn[0], w_gate, w_up, w_down, 0, norm_final, final_norm=False)

    qkv = _qkv(h2, norm_mix[1], attn_w_qkv.astype(BF16), 0, attn_b_qkv[0], q_dim)
    slopes = jnp.exp2(-8.0 * jnp.arange(1, n_heads + 1, dtype=F32) / n_heads)
    o = _attention(qkv, slopes, attn_sinks[0], q_dim, kv_heads)
    h3 = _linear_residual(o, attn_w_o.astype(BF16), 0, h2)
    out = _ffn(h3, norm_ffn[1], w_gate, w_up, w_down, 1, norm_final, final_norm=True)
    return out.reshape(bsz, seq, dm)
```

```python
import functools
import math

import jax
import jax.numpy as jnp
from jax import lax
from jax.experimental import pallas as pl
from jax.experimental.pallas import tpu as pltpu

F32 = jnp.float32
BF16 = jnp.bfloat16

NORM_EPS = 1e-6
S5_GROUP = 16
S5_STATE = 64
S5_T = 16
HEAD_DIM = 64
Q_PER_KV = 8
ATTN_BLOCK = 128
LANES_V7X = 128
SUBLANES_V7X = 8
VMEM_LIMIT_V7X = 56 * 1024 * 1024


def _cparams(sem):
    return pltpu.CompilerParams(dimension_semantics=sem, vmem_limit_bytes=VMEM_LIMIT_V7X)


def _rms(h, gain):
    return h * lax.rsqrt(jnp.mean(h * h, axis=-1, keepdims=True) + NORM_EPS) * gain


GROUPS_PER_TILE = LANES_V7X // S5_GROUP
REGROUP_CHUNKS = 64
REGROUP_PITCH = REGROUP_CHUNKS + 8
REGROUP_UNROLL = 2


def _slot_bits():
    shape = (SUBLANES_V7X, LANES_V7X)
    sub = lax.broadcasted_iota(jnp.int32, shape, 0)
    lane = lax.broadcasted_iota(jnp.int32, shape, 1)
    return ([((sub >> b) & 1) == 1 for b in range(3)],
            [((lane >> (S5_GROUP.bit_length() - 1 + b)) & 1) == 1 for b in range(3)])


def _time_to_group_vreg(x, bits):
    sub_bits, slot_bits = bits
    for b in range(3):
        x = jnp.where(sub_bits[b], pltpu.roll(x, LANES_V7X - (S5_GROUP << b), 1), x)
    for b in range(3):
        x = jnp.where(slot_bits[b], pltpu.roll(x, 1 << b, 0), x)
    return x


def _group_to_time_vreg(z, bits):
    sub_bits, slot_bits = bits
    for b in range(3):
        z = jnp.where(slot_bits[b], pltpu.roll(z, SUBLANES_V7X - (1 << b), 0), z)
    for b in range(3):
        z = jnp.where(sub_bits[b], pltpu.roll(z, S5_GROUP << b, 1), z)
    return z


def _norm_regroup_kernel(h_ref, g_ref, o_ref, inv_ref, z_ref):
    jb, pitch = REGROUP_CHUNKS, REGROUP_PITCH
    ntile = h_ref.shape[1] // LANES_V7X
    h = h_ref[...]
    inv = lax.rsqrt(jnp.mean(h * h, axis=-1, keepdims=True) + NORM_EPS)
    inv_ref[...] = jnp.broadcast_to(inv, inv_ref.shape)
    bits = _slot_bits()

    def body(j, _):
        for half in range(2):
            r0 = pl.multiple_of(j * S5_T + half * SUBLANES_V7X, SUBLANES_V7X)
            scale = inv_ref[pl.ds(r0, SUBLANES_V7X), :]
            for t in range(ntile):
                lanes = slice(t * LANES_V7X, (t + 1) * LANES_V7X)
                x = h_ref[pl.ds(r0, SUBLANES_V7X), lanes] * scale * g_ref[:, lanes]
                z_ref[2 * t + half, pl.ds(j, SUBLANES_V7X, stride=pitch), :] = _time_to_group_vreg(x, bits)
        return 0

    lax.fori_loop(0, jb, body, 0, unroll=REGROUP_UNROLL)
    for g in range(o_ref.shape[0]):
        t, g8 = divmod(g, GROUPS_PER_TILE)
        rows = slice(g8 * pitch, g8 * pitch + jb)
        o_ref[g] = jnp.concatenate([z_ref[2 * t, rows, :], z_ref[2 * t + 1, rows, :]],
                                   axis=-1).astype(o_ref.dtype)


def _norm_regroup(h, gain):
    n, d = h.shape
    tm = REGROUP_CHUNKS * S5_T
    ng = d // S5_GROUP
    return pl.pallas_call(
        _norm_regroup_kernel,
        grid=(n // tm,),
        in_specs=[pl.BlockSpec((tm, d), lambda i: (i, 0)),
                  pl.BlockSpec((1, d), lambda i: (0, 0))],
        out_specs=pl.BlockSpec((ng, REGROUP_CHUNKS, S5_T * S5_GROUP), lambda i: (0, i, 0)),
        out_shape=jax.ShapeDtypeStruct((ng, n // S5_T, S5_T * S5_GROUP), BF16),
        scratch_shapes=[pltpu.VMEM((tm, LANES_V7X), F32),
                        pltpu.VMEM((2 * d // LANES_V7X, GROUPS_PER_TILE * REGROUP_PITCH, LANES_V7X), F32)],
        compiler_params=_cparams(("parallel",)),
        name="norm_regroup",
    )(h, gain.reshape(1, d))


SCAN_SHIFTS = (1, 2, 4)
SCAN_TABLE_ROWS = 2 * SUBLANES_V7X * (len(SCAN_SHIFTS) + 1)


def _scan_rows(block):
    return slice(block * SUBLANES_V7X, (block + 1) * SUBLANES_V7X)


def _s5_kernel(x_ref, m_ref, ws_ref, wo_ref, sc_ref, d_ref, o_ref, s_ref, h_ref, *, gb, nchunk):
    for g in range(gb):
        s_ref[g] = jnp.dot(x_ref[g], ws_ref[g], preferred_element_type=F32)

    row = lax.broadcasted_iota(jnp.int32, (SUBLANES_V7X, LANES_V7X), 0)
    zero = jnp.zeros((SUBLANES_V7X, LANES_V7X), F32)

    def body(i, carry):
        r0 = pl.multiple_of(i * SUBLANES_V7X, SUBLANES_V7X)
        out = []
        for g in range(gb):
            cb, cbs = carry[g]
            z = s_ref[g, pl.ds(r0, SUBLANES_V7X), 0:LANES_V7X]
            zs = s_ref[g, pl.ds(r0, SUBLANES_V7X), LANES_V7X:2 * LANES_V7X]
            for qi, q in enumerate(SCAN_SHIFTS):
                a1 = sc_ref[g, _scan_rows(2 * qi), :]
                a2 = sc_ref[g, _scan_rows(2 * qi + 1), :]
                zr = pltpu.roll(z, q, axis=0)
                zsr = pltpu.roll(zs, q, axis=0)
                z, zs = z + a1 * zr + a2 * zsr, zs + a1 * zsr - a2 * zr
            p1 = sc_ref[g, _scan_rows(2 * len(SCAN_SHIFTS)), :]
            p2 = sc_ref[g, _scan_rows(2 * len(SCAN_SHIFTS) + 1), :]
            hinc = z + p1 * cb + p2 * cbs
            hincs = zs + p1 * cbs - p2 * cb
            h_ref[g, pl.ds(r0, SUBLANES_V7X), :] = jnp.where(row == 0, cb, pltpu.roll(hinc, 1, axis=0))
            last = slice(SUBLANES_V7X - 1, SUBLANES_V7X)
            out.append((jnp.broadcast_to(hinc[last, :], (SUBLANES_V7X, LANES_V7X)),
                        jnp.broadcast_to(hincs[last, :], (SUBLANES_V7X, LANES_V7X))))
        return tuple(out)

    lax.fori_loop(0, nchunk // SUBLANES_V7X, body, tuple((zero, zero) for _ in range(gb)))

    for g in range(gb):
        x = x_ref[g]
        y = jnp.dot(x, m_ref[g], preferred_element_type=F32)
        y = y + jnp.dot(h_ref[g].astype(BF16), wo_ref[g], preferred_element_type=F32)
        y = y + d_ref[g] * x.astype(F32)
        o_ref[g] = jax.nn.gelu(y).astype(o_ref.dtype)


def _s5_core(xg, m, ws, wo, sc, dt, gb=8):
    ng, nchunk, w = xg.shape
    spec3 = lambda a, b: pl.BlockSpec((gb, a, b), lambda i: (i, 0, 0))
    return pl.pallas_call(
        functools.partial(_s5_kernel, gb=gb, nchunk=nchunk),
        grid=(ng // gb,),
        in_specs=[spec3(nchunk, w), spec3(w, w), spec3(w, w), spec3(2 * S5_STATE, w),
                  spec3(SCAN_TABLE_ROWS, LANES_V7X), spec3(1, w)],
        out_specs=spec3(nchunk, w),
        out_shape=jax.ShapeDtypeStruct((ng, nchunk, w), BF16),
        scratch_shapes=[pltpu.VMEM((gb, nchunk, w), F32),
                        pltpu.VMEM((gb, nchunk, 2 * S5_STATE), F32)],
        compiler_params=_cparams(("parallel",)),
        name="s5_core",
    )(xg, m, ws, wo, sc, dt)


def _s5_tables(a_re, a_im, log_step, b_re, b_im, c_re, c_im, d_skip):
    ng, ns = a_re.shape
    t = S5_T
    dt = jnp.exp(log_step)[:, None, None]
    lr, li = a_re[:, None, :], a_im[:, None, :]

    def powers(e):
        ee = e.astype(F32)[None, :, None]
        mag = jnp.exp(lr * dt * ee)
        ang = li * dt * ee
        return mag * jnp.cos(ang), mag * jnp.sin(ang)

    lb_re, lb_im = powers(jnp.arange(1, 2))
    n_re, n_im = lb_re - 1.0, lb_im
    den = lr * lr + li * li
    q_re = ((n_re * lr + n_im * li) / den)[:, 0, :, None]
    q_im = ((n_im * lr - n_re * li) / den)[:, 0, :, None]
    bb_re = q_re * b_re - q_im * b_im
    bb_im = q_re * b_im + q_im * b_re

    def c_times(pr, pi):
        cr, ci = c_re[:, None], c_im[:, None]
        pr, pi = pr[:, :, None, :], pi[:, :, None, :]
        return cr * pr - ci * pi, cr * pi + ci * pr

    pr0, pi0 = powers(jnp.arange(0, t))
    pr0, pi0 = pr0.transpose(0, 2, 1)[..., None], pi0.transpose(0, 2, 1)[..., None]
    ct_re, ct_im = c_re.transpose(0, 2, 1)[:, :, None, :], c_im.transpose(0, 2, 1)[:, :, None, :]
    ca_re = (ct_re * pr0 - ct_im * pi0).reshape(ng, 1, ns, t * S5_GROUP)
    ca_im = (ct_re * pi0 + ct_im * pr0).reshape(ng, 1, ns, t * S5_GROUP)
    bt_re, bt_im = bb_re.transpose(0, 2, 1)[..., None], bb_im.transpose(0, 2, 1)[..., None]
    kern = jnp.sum(ca_re * bt_re - ca_im * bt_im, axis=2).reshape(ng, S5_GROUP, t, S5_GROUP)
    lag = jnp.arange(t)[None, :] - jnp.arange(t)[:, None]
    m = jnp.where((lag >= 0)[None, None, :, :, None], kern[:, :, jnp.clip(lag, 0, t - 1)], 0.0)
    m = m.transpose(0, 2, 1, 3, 4).reshape(ng, t * S5_GROUP, t * S5_GROUP)

    pr, pi = powers(jnp.arange(t - 1, -1, -1))
    pr, pi = pr[:, :, None, :], pi[:, :, None, :]
    br, bi = bb_re.transpose(0, 2, 1)[:, None], bb_im.transpose(0, 2, 1)[:, None]
    ws_re, ws_im = pr * br - pi * bi, pr * bi + pi * br
    ws = jnp.concatenate([ws_re, ws_im, ws_im, ws_re], axis=-1).reshape(ng, t * S5_GROUP, 4 * ns)

    co_re, co_im = c_times(*powers(jnp.arange(1, t + 1)))
    wo = jnp.concatenate([co_re.transpose(0, 3, 1, 2), -co_im.transpose(0, 3, 1, 2)], axis=1)
    wo = wo.reshape(ng, 2 * ns, t * S5_GROUP)

    def pair(e, rows_from):
        er, ei = powers(e)
        keep = (jnp.arange(SUBLANES_V7X) >= rows_from)[None, :, None]
        a1 = jnp.where(keep, jnp.concatenate([er, er], axis=-1), 0.0)
        a2 = jnp.where(keep, jnp.concatenate([-ei, ei], axis=-1), 0.0)
        return [a1, a2]

    sc = []
    for q in SCAN_SHIFTS:
        sc += pair(jnp.full((SUBLANES_V7X,), t * q), q)
    sc += pair(t * jnp.arange(1, SUBLANES_V7X + 1), 0)
    sc = jnp.concatenate(sc, axis=1)

    d_t = jnp.tile(d_skip.reshape(ng, 1, S5_GROUP), (1, 1, t))

    r = jnp.arange(t * S5_GROUP)
    half, slot, chan = r // LANES_V7X, (r % LANES_V7X) // S5_GROUP, r % S5_GROUP
    g8 = jnp.arange(ng)[:, None] % GROUPS_PER_TILE
    step = SUBLANES_V7X * half[None] + (g8 - slot[None]) % SUBLANES_V7X
    idx = step * S5_GROUP + chan[None]
    perm = (idx[:, None, :] == r[None, :, None]).astype(BF16)
    m, ws, wo = m.astype(BF16), ws.astype(BF16), wo.astype(BF16)
    m = jnp.einsum("gab,gbq->gaq", m, perm, preferred_element_type=F32).astype(BF16)
    m = jnp.einsum("gap,gaq->gpq", perm, m, preferred_element_type=F32).astype(BF16)
    ws = jnp.einsum("gap,gan->gpn", perm, ws, preferred_element_type=F32).astype(BF16)
    wo = jnp.einsum("gna,gaq->gnq", wo, perm, preferred_element_type=F32).astype(BF16)
    return m, ws, wo, sc, d_t


def _glu_kernel(yg_ref, wa_ref, wb_ref, x_ref, o_ref, z_ref, lhs_ref):
    jb, pitch = REGROUP_CHUNKS, REGROUP_PITCH
    ntile = lhs_ref.shape[1] // LANES_V7X

    @pl.when(pl.program_id(1) == 0)
    def _():
        for g in range(yg_ref.shape[0]):
            t, g8 = divmod(g, GROUPS_PER_TILE)
            y = yg_ref[g].astype(F32)
            rows = slice(g8 * pitch, g8 * pitch + jb)
            z_ref[2 * t, rows, :] = y[:, :LANES_V7X]
            z_ref[2 * t + 1, rows, :] = y[:, LANES_V7X:]
        bits = _slot_bits()

        def body(j, _):
            r0 = pl.multiple_of(j * S5_T, S5_T)
            for t in range(ntile):
                halves = [_group_to_time_vreg(z_ref[2 * t + half, pl.ds(j, SUBLANES_V7X, stride=pitch), :], bits)
                          for half in range(2)]
                lhs_ref[pl.ds(r0, S5_T), t * LANES_V7X:(t + 1) * LANES_V7X] = (
                    jnp.concatenate(halves, axis=0).astype(lhs_ref.dtype))
            return 0

        lax.fori_loop(0, jb, body, 0, unroll=REGROUP_UNROLL)

    g = lhs_ref[...]
    a = jnp.dot(g, wa_ref[...], preferred_element_type=F32)
    b = jnp.dot(g, wb_ref[...], preferred_element_type=F32)
    o_ref[...] = x_ref[...] + a / (1.0 + jnp.exp(-b))


def _glu_residual(yg, w, layer, x, tn=512):
    ng, nchunk, width = yg.shape
    n, d = x.shape
    k = ng * S5_GROUP
    tm = REGROUP_CHUNKS * S5_T
    nj = d // tn
    return pl.pallas_call(
        _glu_kernel,
        grid=(n // tm, nj),
        in_specs=[pl.BlockSpec((ng, REGROUP_CHUNKS, width), lambda i, j: (0, i, 0)),
                  pl.BlockSpec((None, k, tn), lambda i, j: (layer, 0, j)),
                  pl.BlockSpec((None, k, tn), lambda i, j: (layer, 0, j + nj)),
                  pl.BlockSpec((tm, tn), lambda i, j: (i, j))],
        out_specs=pl.BlockSpec((tm, tn), lambda i, j: (i, j)),
        out_shape=jax.ShapeDtypeStruct((n, d), F32),
        scratch_shapes=[pltpu.VMEM((2 * k // LANES_V7X, GROUPS_PER_TILE * REGROUP_PITCH, LANES_V7X), F32),
                        pltpu.VMEM((tm, k), BF16)],
        compiler_params=_cparams(("parallel", "arbitrary")),
        name="glu_residual",
    )(yg, w, w, x)


def _linres_kernel(a_ref, w_ref, r_ref, o_ref):
    o_ref[...] = r_ref[...] + jnp.dot(a_ref[...], w_ref[...], preferred_element_type=F32)


def _linear_residual(a, w, layer, r, tm=1024, tn=2048):
    n, k = a.shape
    d = w.shape[2]
    return pl.pallas_call(
        _linres_kernel,
        grid=(n // tm, d // tn),
        in_specs=[pl.BlockSpec((tm, k), lambda i, j: (i, 0)),
                  pl.BlockSpec((None, k, tn), lambda i, j: (layer, 0, j)),
                  pl.BlockSpec((tm, tn), lambda i, j: (i, j))],
        out_specs=pl.BlockSpec((tm, tn), lambda i, j: (i, j)),
        out_shape=jax.ShapeDtypeStruct((n, d), F32),
        compiler_params=_cparams(("parallel", "arbitrary")),
        name="linear_residual",
    )(a, w, r)


def _ffn_kernel(h_ref, gn_ref, wg_ref, wu_ref, wd_ref, gf_ref, o_ref, hn_ref, *, final_norm):
    j = pl.program_id(1)

    @pl.when(j == 0)
    def _():
        h = h_ref[...]
        hn_ref[...] = _rms(h, gn_ref[...]).astype(hn_ref.dtype)
        o_ref[...] = h

    hn = hn_ref[...]
    gate = jnp.dot(hn, wg_ref[...], preferred_element_type=F32)
    up = jnp.dot(hn, wu_ref[...], preferred_element_type=F32)
    act = (gate / (1.0 + jnp.exp(-gate)) * up).astype(BF16)
    o_ref[...] += jnp.dot(act, wd_ref[...], preferred_element_type=F32)

    if final_norm:
        @pl.when(j == pl.num_programs(1) - 1)
        def _():
            o_ref[...] = _rms(o_ref[...], gf_ref[...])


def _ffn(h, gain, wg, wu, wd, layer, gain_final, final_norm, tm=1024, tf=512):
    n, d = h.shape
    f = wg.shape[2]
    return pl.pallas_call(
        functools.partial(_ffn_kernel, final_norm=final_norm),
        grid=(n // tm, f // tf),
        in_specs=[pl.BlockSpec((tm, d), lambda i, j: (i, 0)),
                  pl.BlockSpec((1, d), lambda i, j: (0, 0)),
                  pl.BlockSpec((None, d, tf), lambda i, j: (layer, 0, j)),
                  pl.BlockSpec((None, d, tf), lambda i, j: (layer, 0, j)),
                  pl.BlockSpec((None, tf, d), lambda i, j: (layer, j, 0)),
                  pl.BlockSpec((1, d), lambda i, j: (0, 0))],
        out_specs=pl.BlockSpec((tm, d), lambda i, j: (i, 0)),
        out_shape=jax.ShapeDtypeStruct((n, d), F32),
        scratch_shapes=[pltpu.VMEM((tm, d), BF16)],
        compiler_params=_cparams(("parallel", "arbitrary")),
        name="ffn_final" if final_norm else "ffn",
    )(h, gain.reshape(1, d), wg, wu, wd, gain_final.reshape(1, d))


def _qkv_kernel(h_ref, gn_ref, w_ref, b_ref, s_ref, o_ref, hn_ref):
    @pl.when(pl.program_id(1) == 0)
    def _():
        hn_ref[...] = _rms(h_ref[...], gn_ref[...]).astype(hn_ref.dtype)

    acc = jnp.dot(hn_ref[...], w_ref[...], preferred_element_type=F32)
    o_ref[...] = ((acc + b_ref[...]) * s_ref[...]).astype(o_ref.dtype)


def _qkv(h, gain, w, layer, b, q_dim, tm=1024, tn=2560):
    n, d = h.shape
    m = w.shape[2]
    assert math.log2(HEAD_DIM).is_integer() and int(math.log2(HEAD_DIM)) % 2 == 0
    col_scale = jnp.where(jnp.arange(m) < q_dim, HEAD_DIM ** -0.5, 1.0).astype(F32)
    return pl.pallas_call(
        _qkv_kernel,
        grid=(n // tm, m // tn),
        in_specs=[pl.BlockSpec((tm, d), lambda i, j: (i, 0)),
                  pl.BlockSpec((1, d), lambda i, j: (0, 0)),
                  pl.BlockSpec((None, d, tn), lambda i, j: (layer, 0, j)),
                  pl.BlockSpec((1, tn), lambda i, j: (0, j)),
                  pl.BlockSpec((1, tn), lambda i, j: (0, j))],
        out_specs=pl.BlockSpec((tm, tn), lambda i, j: (i, j)),
        out_shape=jax.ShapeDtypeStruct((n, m), BF16),
        scratch_shapes=[pltpu.VMEM((tm, d), BF16)],
        compiler_params=_cparams(("parallel", "arbitrary")),
        name="qkv",
    )(h, gain.reshape(1, d), w, b.reshape(1, m), col_scale.reshape(1, m))


def _attn_kernel(slope_ref, sink_ref, q_ref, kv_ref, kvp_ref, o_ref, kbuf_ref, *, tq, kv_heads):
    blk = ATTN_BLOCK
    kvw = kv_heads * HEAD_DIM
    step = pl.program_id(0)
    kbuf_ref[0:blk, :] = kvp_ref[...]
    kbuf_ref[blk:, :] = kv_ref[...]

    kj = lax.broadcasted_iota(jnp.int32, (blk, blk), 0)
    qi = lax.broadcasted_iota(jnp.int32, (blk, blk), 1)
    own = kj <= qi
    neg_dist = -jnp.where(own, qi - kj, qi + blk - kj).astype(F32)

    def body(b, _):
        r0 = pl.multiple_of(b * blk, blk)
        first = jnp.logical_and(step == 0, b == 0)
        nd = jnp.where(jnp.logical_or(own, jnp.logical_not(first)), neg_dist, -jnp.inf)
        for kvh in range(kv_heads):
            k = kbuf_ref[pl.ds(r0, 2 * blk), kvh * HEAD_DIM:(kvh + 1) * HEAD_DIM]
            v = kbuf_ref[pl.ds(r0, 2 * blk), kvw + kvh * HEAD_DIM:kvw + (kvh + 1) * HEAD_DIM]
            heads = [kvh * Q_PER_KV + g for g in range(Q_PER_KV)]
            q8 = jnp.concatenate(
                [q_ref[pl.ds(r0, blk), h * HEAD_DIM:(h + 1) * HEAD_DIM] for h in heads], axis=0)
            st = lax.dot_general(k, q8, (((1,), (1,)), ((), ())), preferred_element_type=F32)
            ps, dens = [], []
            for g, h in enumerate(heads):
                cols = slice(g * blk, (g + 1) * blk)
                s = jnp.where(own, st[blk:, cols], st[:blk, cols]) + slope_ref[h] * nd
                sink = sink_ref[h]
                mx = jnp.maximum(jnp.max(s, axis=0, keepdims=True), sink)
                p = jnp.exp(s - mx)
                dens.append(jnp.sum(p, axis=0, keepdims=True) + jnp.exp(sink - mx))
                ps.append(jnp.concatenate([jnp.where(own, 0.0, p), jnp.where(own, p, 0.0)], axis=0).astype(BF16))
            ot = lax.dot_general(v, jnp.concatenate(ps, axis=1), (((0,), (0,)), ((), ())),
                                 preferred_element_type=F32)
            o8 = (ot / jnp.concatenate(dens, axis=1)).T
            lo = kvh * Q_PER_KV * HEAD_DIM
            o_ref[pl.ds(r0, blk), lo:lo + Q_PER_KV * HEAD_DIM] = jnp.concatenate(
                [o8[g * blk:(g + 1) * blk] for g in range(Q_PER_KV)], axis=-1).astype(o_ref.dtype)
        return 0

    lax.fori_loop(0, tq // blk, body, 0)


def _attention(qkv, slopes, sinks, q_dim, kv_heads, tq=512):
    n = qkv.shape[0]
    kvw = 2 * kv_heads * HEAD_DIM
    kv_col = q_dim // kvw
    per = tq // ATTN_BLOCK
    return pl.pallas_call(
        functools.partial(_attn_kernel, tq=tq, kv_heads=kv_heads),
        grid=(n // tq,),
        in_specs=[pl.BlockSpec(memory_space=pltpu.SMEM),
                  pl.BlockSpec(memory_space=pltpu.SMEM),
                  pl.BlockSpec((tq, q_dim), lambda i: (i, 0)),
                  pl.BlockSpec((tq, kvw), lambda i: (i, kv_col)),
                  pl.BlockSpec((ATTN_BLOCK, kvw), lambda i: (jnp.maximum(i * per - 1, 0), kv_col))],
        out_specs=pl.BlockSpec((tq, q_dim), lambda i: (i, 0)),
        out_shape=jax.ShapeDtypeStruct((n, q_dim), BF16),
        scratch_shapes=[pltpu.VMEM((tq + ATTN_BLOCK, kvw), BF16)],
        compiler_params=_cparams(("parallel",)),
        name="swa_attention",
    )(slopes, sinks, qkv, qkv, qkv)


def kernel(x, norm_mix, s5_a_re, s5_a_im, s5_log_step, s5_b_re, s5_b_im, s5_c_re, s5_c_im, s5_d, s5_w_glu, attn_w_qkv, attn_b_qkv, attn_sinks, attn_w_o, norm_ffn, ffn_w_gate, ffn_w_up, ffn_w_down, norm_final):
    bsz, seq, dm = x.shape
    ng = dm // S5_GROUP
    nchunk = seq // S5_T
    n_heads = attn_sinks.shape[1]
    q_dim = n_heads * HEAD_DIM
    kv_heads = n_heads // Q_PER_KV
    assert bsz == 1 and norm_mix.shape[0] == 2 and seq % 1024 == 0
    assert S5_T * S5_GROUP == 2 * LANES_V7X and s5_a_re.shape[2] == S5_STATE
    assert q_dim % (2 * kv_heads * HEAD_DIM) == 0

    h0 = x.reshape(seq, dm)

    w_gate, w_up, w_down = (w.astype(BF16) for w in (ffn_w_gate, ffn_w_up, ffn_w_down))
    xg = _norm_regroup(h0, norm_mix[0])
    tables = _s5_tables(s5_a_re[0], s5_a_im[0], s5_log_step[0], s5_b_re[0], s5_b_im[0],
                        s5_c_re[0], s5_c_im[0], s5_d[0])
    yg = _s5_core(xg, *tables)
    h1 = _glu_residual(yg, s5_w_glu.astype(BF16), 0, h0)
    h2 = _ffn(h1, norm_ffn[0], w_gate, w_up, w_down, 0, norm_final, final_norm=False)

    qkv = _qkv(h2, norm_mix[1], attn_w_qkv.astype(BF16), 0, attn_b_qkv[0], q_dim)
    slopes = jnp.exp2(-8.0 * jnp.arange(1, n_heads + 1, dtype=F32) / n_heads)
    o = _attention(qkv, slopes, attn_sinks[0], q_dim, kv_heads)
    h3 = _linear_residual(o, attn_w_o.astype(BF16), 0, h2)
    out = _ffn(h3, norm_ffn[1], w_gate, w_up, w_down, 1, norm_final, final_norm=True)
    return out.reshape(bsz, seq, dm)
```

```python
import functools
import math

import jax
import jax.numpy as jnp
from jax import lax
from jax.experimental import pallas as pl
from jax.experimental.pallas import tpu as pltpu

F32 = jnp.float32
BF16 = jnp.bfloat16

NORM_EPS = 1e-6
S5_GROUP = 16
S5_STATE = 64
S5_T = 16
HEAD_DIM = 64
Q_PER_KV = 8
ATTN_BLOCK = 128
LANES_V7X = 128
SUBLANES_V7X = 8
VMEM_LIMIT_V7X = 56 * 1024 * 1024


def _cparams(sem):
    return pltpu.CompilerParams(dimension_semantics=sem, vmem_limit_bytes=VMEM_LIMIT_V7X)


def _rms(h, gain):
    return h * lax.rsqrt(jnp.mean(h * h, axis=-1, keepdims=True) + NORM_EPS) * gain


GROUPS_PER_TILE = LANES_V7X // S5_GROUP
REGROUP_CHUNKS = 64
REGROUP_PITCH = REGROUP_CHUNKS + 8
REGROUP_UNROLL = 2


def _slot_bits():
    shape = (SUBLANES_V7X, LANES_V7X)
    sub = lax.broadcasted_iota(jnp.int32, shape, 0)
    lane = lax.broadcasted_iota(jnp.int32, shape, 1)
    return ([((sub >> b) & 1) == 1 for b in range(3)],
            [((lane >> (S5_GROUP.bit_length() - 1 + b)) & 1) == 1 for b in range(3)])


def _time_to_group_vreg(x, bits):
    sub_bits, slot_bits = bits
    for b in range(3):
        x = jnp.where(sub_bits[b], pltpu.roll(x, LANES_V7X - (S5_GROUP << b), 1), x)
    for b in range(3):
        x = jnp.where(slot_bits[b], pltpu.roll(x, 1 << b, 0), x)
    return x


def _group_to_time_vreg(z, bits):
    sub_bits, slot_bits = bits
    for b in range(3):
        z = jnp.where(slot_bits[b], pltpu.roll(z, SUBLANES_V7X - (1 << b), 0), z)
    for b in range(3):
        z = jnp.where(sub_bits[b], pltpu.roll(z, S5_GROUP << b, 1), z)
    return z


def _norm_regroup_kernel(h_ref, g_ref, o_ref, inv_ref, z_ref):
    jb, pitch = REGROUP_CHUNKS, REGROUP_PITCH
    ntile = h_ref.shape[1] // LANES_V7X
    h = h_ref[...]
    inv = lax.rsqrt(jnp.mean(h * h, axis=-1, keepdims=True) + NORM_EPS)
    inv_ref[...] = jnp.broadcast_to(inv, inv_ref.shape)
    bits = _slot_bits()

    def body(j, _):
        for half in range(2):
            r0 = pl.multiple_of(j * S5_T + half * SUBLANES_V7X, SUBLANES_V7X)
            scale = inv_ref[pl.ds(r0, SUBLANES_V7X), :]
            for t in range(ntile):
                lanes = slice(t * LANES_V7X, (t + 1) * LANES_V7X)
                x = h_ref[pl.ds(r0, SUBLANES_V7X), lanes] * scale * g_ref[:, lanes]
                z_ref[2 * t + half, pl.ds(j, SUBLANES_V7X, stride=pitch), :] = _time_to_group_vreg(x, bits)
        return 0

    lax.fori_loop(0, jb, body, 0, unroll=REGROUP_UNROLL)
    for g in range(o_ref.shape[0]):
        t, g8 = divmod(g, GROUPS_PER_TILE)
        rows = slice(g8 * pitch, g8 * pitch + jb)
        o_ref[g] = jnp.concatenate([z_ref[2 * t, rows, :], z_ref[2 * t + 1, rows, :]],
                                   axis=-1).astype(o_ref.dtype)


def _norm_regroup(h, gain):
    n, d = h.shape
    tm = REGROUP_CHUNKS * S5_T
    ng = d // S5_GROUP
    return pl.pallas_call(
        _norm_regroup_kernel,
        grid=(n // tm,),
        in_specs=[pl.BlockSpec((tm, d), lambda i: (i, 0)),
                  pl.BlockSpec((1, d), lambda i: (0, 0))],
        out_specs=pl.BlockSpec((ng, REGROUP_CHUNKS, S5_T * S5_GROUP), lambda i: (0, i, 0)),
        out_shape=jax.ShapeDtypeStruct((ng, n // S5_T, S5_T * S5_GROUP), BF16),
        scratch_shapes=[pltpu.VMEM((tm, LANES_V7X), F32),
                        pltpu.VMEM((2 * d // LANES_V7X, GROUPS_PER_TILE * REGROUP_PITCH, LANES_V7X), F32)],
        compiler_params=_cparams(("parallel",)),
        name="norm_regroup",
    )(h, gain.reshape(1, d))


SCAN_SHIFTS = (1, 2, 4)
SCAN_TABLE_ROWS = 2 * SUBLANES_V7X * (len(SCAN_SHIFTS) + 1)


def _scan_rows(block):
    return slice(block * SUBLANES_V7X, (block + 1) * SUBLANES_V7X)


def _s5_kernel(x_ref, m_ref, ws_ref, wo_ref, sc_ref, d_ref, o_ref, s_ref, h_ref, *, gb, nchunk):
    for g in range(gb):
        s_ref[g] = jnp.dot(x_ref[g], ws_ref[g], preferred_element_type=F32)

    row = lax.broadcasted_iota(jnp.int32, (SUBLANES_V7X, LANES_V7X), 0)
    zero = jnp.zeros((SUBLANES_V7X, LANES_V7X), F32)

    def body(i, carry):
        r0 = pl.multiple_of(i * SUBLANES_V7X, SUBLANES_V7X)
        out = []
        for g in range(gb):
            cb, cbs = carry[g]
            z = s_ref[g, pl.ds(r0, SUBLANES_V7X), 0:LANES_V7X]
            zs = s_ref[g, pl.ds(r0, SUBLANES_V7X), LANES_V7X:2 * LANES_V7X]
            for qi, q in enumerate(SCAN_SHIFTS):
                a1 = sc_ref[g, _scan_rows(2 * qi), :]
                a2 = sc_ref[g, _scan_rows(2 * qi + 1), :]
                zr = pltpu.roll(z, q, axis=0)
                zsr = pltpu.roll(zs, q, axis=0)
                z, zs = z + a1 * zr + a2 * zsr, zs + a1 * zsr - a2 * zr
            p1 = sc_ref[g, _scan_rows(2 * len(SCAN_SHIFTS)), :]
            p2 = sc_ref[g, _scan_rows(2 * len(SCAN_SHIFTS) + 1), :]
            hinc = z + p1 * cb + p2 * cbs
            hincs = zs + p1 * cbs - p2 * cb
            h_ref[g, pl.ds(r0, SUBLANES_V7X), :] = jnp.where(row == 0, cb, pltpu.roll(hinc, 1, axis=0))
            last = slice(SUBLANES_V7X - 1, SUBLANES_V7X)
            out.append((jnp.broadcast_to(hinc[last, :], (SUBLANES_V7X, LANES_V7X)),
                        jnp.broadcast_to(hincs[last, :], (SUBLANES_V7X, LANES_V7X))))
        return tuple(out)

    lax.fori_loop(0, nchunk // SUBLANES_V7X, body, tuple((zero, zero) for _ in range(gb)))

    for g in range(gb):
        x = x_ref[g]
        y = jnp.dot(x, m_ref[g], preferred_element_type=F32)
        y = y + jnp.dot(h_ref[g].astype(BF16), wo_ref[g], preferred_element_type=F32)
        y = y + d_ref[g] * x.astype(F32)
        o_ref[g] = jax.nn.gelu(y).astype(o_ref.dtype)


def _s5_core(xg, m, ws, wo, sc, dt, gb=8):
    ng, nchunk, w = xg.shape
    spec3 = lambda a, b: pl.BlockSpec((gb, a, b), lambda i: (i, 0, 0))
    return pl.pallas_call(
        functools.partial(_s5_kernel, gb=gb, nchunk=nchunk),
        grid=(ng // gb,),
        in_specs=[spec3(nchunk, w), spec3(w, w), spec3(w, w), spec3(2 * S5_STATE, w),
                  spec3(SCAN_TABLE_ROWS, LANES_V7X), spec3(1, w)],
        out_specs=spec3(nchunk, w),
        out_shape=jax.ShapeDtypeStruct((ng, nchunk, w), BF16),
        scratch_shapes=[pltpu.VMEM((gb, nchunk, w), F32),
                        pltpu.VMEM((gb, nchunk, 2 * S5_STATE), F32)],
        compiler_params=_cparams(("parallel",)),
        name="s5_core",
    )(xg, m, ws, wo, sc, dt)


def _s5_tables(a_re, a_im, log_step, b_re, b_im, c_re, c_im, d_skip):
    ng, ns = a_re.shape
    t = S5_T
    dt = jnp.exp(log_step)[:, None, None]
    lr, li = a_re[:, None, :], a_im[:, None, :]

    def powers(e):
        ee = e.astype(F32)[None, :, None]
        mag = jnp.exp(lr * dt * ee)
        ang = li * dt * ee
        return mag * jnp.cos(ang), mag * jnp.sin(ang)

    lb_re, lb_im = powers(jnp.arange(1, 2))
    n_re, n_im = lb_re - 1.0, lb_im
    den = lr * lr + li * li
    q_re = ((n_re * lr + n_im * li) / den)[:, 0, :, None]
    q_im = ((n_im * lr - n_re * li) / den)[:, 0, :, None]
    bb_re = q_re * b_re - q_im * b_im
    bb_im = q_re * b_im + q_im * b_re

    def c_times(pr, pi):
        cr, ci = c_re[:, None], c_im[:, None]
        pr, pi = pr[:, :, None, :], pi[:, :, None, :]
        return cr * pr - ci * pi, cr * pi + ci * pr

    pr0, pi0 = powers(jnp.arange(0, t))
    pr0, pi0 = pr0.transpose(0, 2, 1)[..., None], pi0.transpose(0, 2, 1)[..., None]
    ct_re, ct_im = c_re.transpose(0, 2, 1)[:, :, None, :], c_im.transpose(0, 2, 1)[:, :, None, :]
    ca_re = (ct_re * pr0 - ct_im * pi0).reshape(ng, 1, ns, t * S5_GROUP)
    ca_im = (ct_re * pi0 + ct_im * pr0).reshape(ng, 1, ns, t * S5_GROUP)
    bt_re, bt_im = bb_re.transpose(0, 2, 1)[..., None], bb_im.transpose(0, 2, 1)[..., None]
    kern = jnp.sum(ca_re * bt_re - ca_im * bt_im, axis=2).reshape(ng, S5_GROUP, t, S5_GROUP)
    lag = jnp.arange(t)[None, :] - jnp.arange(t)[:, None]
    m = jnp.where((lag >= 0)[None, None, :, :, None], kern[:, :, jnp.clip(lag, 0, t - 1)], 0.0)
    m = m.transpose(0, 2, 1, 3, 4).reshape(ng, t * S5_GROUP, t * S5_GROUP)

    pr, pi = powers(jnp.arange(t - 1, -1, -1))
    pr, pi = pr[:, :, None, :], pi[:, :, None, :]
    br, bi = bb_re.transpose(0, 2, 1)[:, None], bb_im.transpose(0, 2, 1)[:, None]
    ws_re, ws_im = pr * br - pi * bi, pr * bi + pi * br
    ws = jnp.concatenate([ws_re, ws_im, ws_im, ws_re], axis=-1).reshape(ng, t * S5_GROUP, 4 * ns)

    co_re, co_im = c_times(*powers(jnp.arange(1, t + 1)))
    wo = jnp.concatenate([co_re.transpose(0, 3, 1, 2), -co_im.transpose(0, 3, 1, 2)], axis=1)
    wo = wo.reshape(ng, 2 * ns, t * S5_GROUP)

    def pair(e, rows_from):
        er, ei = powers(e)
        keep = (jnp.arange(SUBLANES_V7X) >= rows_from)[None, :, None]
        a1 = jnp.where(keep, jnp.concatenate([er, er], axis=-1), 0.0)
        a2 = jnp.where(keep, jnp.concatenate([-ei, ei], axis=-1), 0.0)
        return [a1, a2]

    sc = []
    for q in SCAN_SHIFTS:
        sc += pair(jnp.full((SUBLANES_V7X,), t * q), q)
    sc += pair(t * jnp.arange(1, SUBLANES_V7X + 1), 0)
    sc = jnp.concatenate(sc, axis=1)

    d_t = jnp.tile(d_skip.reshape(ng, 1, S5_GROUP), (1, 1, t))

    r = jnp.arange(t * S5_GROUP)
    half, slot, chan = r // LANES_V7X, (r % LANES_V7X) // S5_GROUP, r % S5_GROUP
    g8 = jnp.arange(ng)[:, None] % GROUPS_PER_TILE
    step = SUBLANES_V7X * half[None] + (g8 - slot[None]) % SUBLANES_V7X
    idx = step * S5_GROUP + chan[None]
    perm = (idx[:, None, :] == r[None, :, None]).astype(BF16)
    m, ws, wo = m.astype(BF16), ws.astype(BF16), wo.astype(BF16)
    m = jnp.einsum("gab,gbq->gaq", m, perm, preferred_element_type=F32).astype(BF16)
    m = jnp.einsum("gap,gaq->gpq", perm, m, preferred_element_type=F32).astype(BF16)
    ws = jnp.einsum("gap,gan->gpn", perm, ws, preferred_element_type=F32).astype(BF16)
    wo = jnp.einsum("gna,gaq->gnq", wo, perm, preferred_element_type=F32).astype(BF16)
    return m, ws, wo, sc, d_t


def _glu_kernel(yg_ref, wa_ref, wb_ref, x_ref, o_ref, z_ref, lhs_ref):
    jb, pitch = REGROUP_CHUNKS, REGROUP_PITCH
    ntile = lhs_ref.shape[1] // LANES_V7X

    @pl.when(pl.program_id(1) == 0)
    def _():
        for g in range(yg_ref.shape[0]):
            t, g8 = divmod(g, GROUPS_PER_TILE)
            y = yg_ref[g].astype(F32)
            rows = slice(g8 * pitch, g8 * pitch + jb)
            z_ref[2 * t, rows, :] = y[:, :LANES_V7X]
            z_ref[2 * t + 1, rows, :] = y[:, LANES_V7X:]
        bits = _slot_bits()

        def body(j, _):
            r0 = pl.multiple_of(j * S5_T, S5_T)
            for t in range(ntile):
                halves = [_group_to_time_vreg(z_ref[2 * t + half, pl.ds(j, SUBLANES_V7X, stride=pitch), :], bits)
                          for half in range(2)]
                lhs_ref[pl.ds(r0, S5_T), t * LANES_V7X:(t + 1) * LANES_V7X] = (
                    jnp.concatenate(halves, axis=0).astype(lhs_ref.dtype))
            return 0

        lax.fori_loop(0, jb, body, 0, unroll=REGROUP_UNROLL)

    g = lhs_ref[...]
    a = jnp.dot(g, wa_ref[...], preferred_element_type=F32)
    b = jnp.dot(g, wb_ref[...], preferred_element_type=F32)
    o_ref[...] = x_ref[...] + a / (1.0 + jnp.exp(-b))


def _glu_residual(yg, w, layer, x, tn=512):
    ng, nchunk, width = yg.shape
    n, d = x.shape
    k = ng * S5_GROUP
    tm = REGROUP_CHUNKS * S5_T
    nj = d // tn
    return pl.pallas_call(
        _glu_kernel,
        grid=(n // tm, nj),
        in_specs=[pl.BlockSpec((ng, REGROUP_CHUNKS, width), lambda i, j: (0, i, 0)),
                  pl.BlockSpec((None, k, tn), lambda i, j: (layer, 0, j)),
                  pl.BlockSpec((None, k, tn), lambda i, j: (layer, 0, j + nj)),
                  pl.BlockSpec((tm, tn), lambda i, j: (i, j))],
        out_specs=pl.BlockSpec((tm, tn), lambda i, j: (i, j)),
        out_shape=jax.ShapeDtypeStruct((n, d), F32),
        scratch_shapes=[pltpu.VMEM((2 * k // LANES_V7X, GROUPS_PER_TILE * REGROUP_PITCH, LANES_V7X), F32),
                        pltpu.VMEM((tm, k), BF16)],
        compiler_params=_cparams(("parallel", "arbitrary")),
        name="glu_residual",
    )(yg, w, w, x)


def _linres_kernel(a_ref, w_ref, r_ref, o_ref):
    o_ref[...] = r_ref[...] + jnp.dot(a_ref[...], w_ref[...], preferred_element_type=F32)


def _linear_residual(a, w, layer, r, tm=1024, tn=2048):
    n, k = a.shape
    d = w.shape[2]
    return pl.pallas_call(
        _linres_kernel,
        grid=(n // tm, d // tn),
        in_specs=[pl.BlockSpec((tm, k), lambda i, j: (i, 0)),
                  pl.BlockSpec((None, k, tn), lambda i, j: (layer, 0, j)),
                  pl.BlockSpec((tm, tn), lambda i, j: (i, j))],
        out_specs=pl.BlockSpec((tm, tn), lambda i, j: (i, j)),
        out_shape=jax.ShapeDtypeStruct((n, d), F32),
        compiler_params=_cparams(("parallel", "arbitrary")),
        name="linear_residual",
    )(a, w, r)


def _ffn_kernel(h_ref, gn_ref, wg_ref, wu_ref, wd_ref, gf_ref, o_ref, hn_ref, *, final_norm):
    j = pl.program_id(1)

    @pl.when(j == 0)
    def _():
        h = h_ref[...]
        hn_ref[...] = _rms(h, gn_ref[...]).astype(hn_ref.dtype)
        o_ref[...] = h

    hn = hn_ref[...]
    gate = jnp.dot(hn, wg_ref[...], preferred_element_type=F32)
    up = jnp.dot(hn, wu_ref[...], preferred_element_type=F32)
    act = (gate / (1.0 + jnp.exp(-gate)) * up).astype(BF16)
    o_ref[...] += jnp.dot(act, wd_ref[...], preferred_element_type=F32)

    if final_norm:
        @pl.when(j == pl.num_programs(1) - 1)
        def _():
            o_ref[...] = _rms(o_ref[...], gf_ref[...])


def _ffn(h, gain, wg, wu, wd, layer, gain_final, final_norm, tm=1024, tf=512):
    n, d = h.shape
    f = wg.shape[2]
    return pl.pallas_call(
        functools.partial(_ffn_kernel, final_norm=final_norm),
        grid=(n // tm, f // tf),
        in_specs=[pl.BlockSpec((tm, d), lambda i, j: (i, 0)),
                  pl.BlockSpec((1, d), lambda i, j: (0, 0)),
                  pl.BlockSpec((None, d, tf), lambda i, j: (layer, 0, j)),
                  pl.BlockSpec((None, d, tf), lambda i, j: (layer, 0, j)),
                  pl.BlockSpec((None, tf, d), lambda i, j: (layer, j, 0)),
                  pl.BlockSpec((1, d), lambda i, j: (0, 0))],
        out_specs=pl.BlockSpec((tm, d), lambda i, j: (i, 0)),
        out_shape=jax.ShapeDtypeStruct((n, d), F32),
        scratch_shapes=[pltpu.VMEM((tm, d), BF16)],
        compiler_params=_cparams(("parallel", "arbitrary")),
        name="ffn_final" if final_norm else "ffn",
    )(h, gain.reshape(1, d), wg, wu, wd, gain_final.reshape(1, d))


def _qkv_kernel(h_ref, gn_ref, w_ref, b_ref, s_ref, o_ref, hn_ref):
    @pl.when(pl.program_id(1) == 0)
    def _():
        hn_ref[...] = _rms(h_ref[...], gn_ref[...]).astype(hn_ref.dtype)

    acc = jnp.dot(hn_ref[...], w_ref[...], preferred_element_type=F32)
    o_ref[...] = ((acc + b_ref[...]) * s_ref[...]).astype(o_ref.dtype)


def _qkv(h, gain, w, layer, b, q_dim, tm=1024, tn=2560):
    n, d = h.shape
    m = w.shape[2]
    assert math.log2(HEAD_DIM).is_integer() and int(math.log2(HEAD_DIM)) % 2 == 0
    col_scale = jnp.where(jnp.arange(m) < q_dim, HEAD_DIM ** -0.5, 1.0).astype(F32)
    return pl.pallas_call(
        _qkv_kernel,
        grid=(n // tm, m // tn),
        in_specs=[pl.BlockSpec((tm, d), lambda i, j: (i, 0)),
                  pl.BlockSpec((1, d), lambda i, j: (0, 0)),
                  pl.BlockSpec((None, d, tn), lambda i, j: (layer, 0, j)),
                  pl.BlockSpec((1, tn), lambda i, j: (0, j)),
                  pl.BlockSpec((1, tn), lambda i, j: (0, j))],
        out_specs=pl.BlockSpec((tm, tn), lambda i, j: (i, j)),
        out_shape=jax.ShapeDtypeStruct((n, m), BF16),
        scratch_shapes=[pltpu.VMEM((tm, d), BF16)],
        compiler_params=_cparams(("parallel", "arbitrary")),
        name="qkv",
    )(h, gain.reshape(1, d), w, b.reshape(1, m), col_scale.reshape(1, m))


def _attn_kernel(slope_ref, sink_ref, q_ref, kv_ref, kvp_ref, o_ref, kbuf_ref, *, tq, kv_heads):
    blk = ATTN_BLOCK
    kvw = kv_heads * HEAD_DIM
    step = pl.program_id(0)
    kbuf_ref[0:blk, :] = kvp_ref[...]
    kbuf_ref[blk:, :] = kv_ref[...]

    kj = lax.broadcasted_iota(jnp.int32, (blk, blk), 0)
    qi = lax.broadcasted_iota(jnp.int32, (blk, blk), 1)
    own = kj <= qi
    neg_dist = -jnp.where(own, qi - kj, qi + blk - kj).astype(F32)

    def body(b, _):
        r0 = pl.multiple_of(b * blk, blk)
        first = jnp.logical_and(step == 0, b == 0)
        nd = jnp.where(jnp.logical_or(own, jnp.logical_not(first)), neg_dist, -jnp.inf)
        for kvh in range(kv_heads):
            k = kbuf_ref[pl.ds(r0, 2 * blk), kvh * HEAD_DIM:(kvh + 1) * HEAD_DIM]
            v = kbuf_ref[pl.ds(r0, 2 * blk), kvw + kvh * HEAD_DIM:kvw + (kvh + 1) * HEAD_DIM]
            heads = [kvh * Q_PER_KV + g for g in range(Q_PER_KV)]
            q8 = jnp.concatenate(
                [q_ref[pl.ds(r0, blk), h * HEAD_DIM:(h + 1) * HEAD_DIM] for h in heads], axis=0)
            st = lax.dot_general(k, q8, (((1,), (1,)), ((), ())), preferred_element_type=F32)
            ps, dens = [], []
            for g, h in enumerate(heads):
                cols = slice(g * blk, (g + 1) * blk)
                s = jnp.where(own, st[blk:, cols], st[:blk, cols]) + slope_ref[h] * nd
                sink = sink_ref[h]
                mx = jnp.maximum(jnp.max(s, axis=0, keepdims=True), sink)
                p = jnp.exp(s - mx)
                dens.append(jnp.sum(p, axis=0, keepdims=True) + jnp.exp(sink - mx))
                ps.append(jnp.concatenate([jnp.where(own, 0.0, p), jnp.where(own, p, 0.0)], axis=0).astype(BF16))
            ot = lax.dot_general(v, jnp.concatenate(ps, axis=1), (((0,), (0,)), ((), ())),
                                 preferred_element_type=F32)
            o8 = (ot / jnp.concatenate(dens, axis=1)).T
            lo = kvh * Q_PER_KV * HEAD_DIM
            o_ref[pl.ds(r0, blk), lo:lo + Q_PER_KV * HEAD_DIM] = jnp.concatenate(
                [o8[g * blk:(g + 1) * blk] for g in range(Q_PER_KV)], axis=-1).astype(o_ref.dtype)
        return 0

    lax.fori_loop(0, tq // blk, body, 0, unroll=True)


def _attention(qkv, slopes, sinks, q_dim, kv_heads, tq=512):
    n = qkv.shape[0]
    kvw = 2 * kv_heads * HEAD_DIM
    kv_col = q_dim // kvw
    per = tq // ATTN_BLOCK
    return pl.pallas_call(
        functools.partial(_attn_kernel, tq=tq, kv_heads=kv_heads),
        grid=(n // tq,),
        in_specs=[pl.BlockSpec(memory_space=pltpu.SMEM),
                  pl.BlockSpec(memory_space=pltpu.SMEM),
                  pl.BlockSpec((tq, q_dim), lambda i: (i, 0)),
                  pl.BlockSpec((tq, kvw), lambda i: (i, kv_col)),
                  pl.BlockSpec((ATTN_BLOCK, kvw), lambda i: (jnp.maximum(i * per - 1, 0), kv_col))],
        out_specs=pl.BlockSpec((tq, q_dim), lambda i: (i, 0)),
        out_shape=jax.ShapeDtypeStruct((n, q_dim), BF16),
        scratch_shapes=[pltpu.VMEM((tq + ATTN_BLOCK, kvw), BF16)],
        compiler_params=_cparams(("parallel",)),
        name="swa_attention",
    )(slopes, sinks, qkv, qkv, qkv)


def kernel(x, norm_mix, s5_a_re, s5_a_im, s5_log_step, s5_b_re, s5_b_im, s5_c_re, s5_c_im, s5_d, s5_w_glu, attn_w_qkv, attn_b_qkv, attn_sinks, attn_w_o, norm_ffn, ffn_w_gate, ffn_w_up, ffn_w_down, norm_final):
    bsz, seq, dm = x.shape
    ng = dm // S5_GROUP
    nchunk = seq // S5_T
    n_heads = attn_sinks.shape[1]
    q_dim = n_heads * HEAD_DIM
    kv_heads = n_heads // Q_PER_KV
    assert bsz == 1 and norm_mix.shape[0] == 2 and seq % 1024 == 0
    assert S5_T * S5_GROUP == 2 * LANES_V7X and s5_a_re.shape[2] == S5_STATE
    assert q_dim % (2 * kv_heads * HEAD_DIM) == 0

    h0 = x.reshape(seq, dm)

    w_gate, w_up, w_down = (w.astype(BF16) for w in (ffn_w_gate, ffn_w_up, ffn_w_down))
    xg = _norm_regroup(h0, norm_mix[0])
    tables = _s5_tables(s5_a_re[0], s5_a_im[0], s5_log_step[0], s5_b_re[0], s5_b_im[0],
                        s5_c_re[0], s5_c_im[0], s5_d[0])
    yg = _s5_core(xg, *tables)
    h1 = _glu_residual(yg, s5_w_glu.astype(BF16), 0, h0)
    h2 = _ffn(h1, norm_ffn[0], w_gate, w_up, w_down, 0, norm_final, final_norm=False)

    qkv = _qkv(h2, norm_mix[1], attn_w_qkv.astype(BF16), 0, attn_b_qkv[0], q_dim)
    slopes = jnp.exp2(-8.0 * jnp.arange(1, n_heads + 1, dtype=F32) / n_heads)
    o = _attention(qkv, slopes, attn_sinks[0], q_dim, kv_heads)
    h3 = _linear_residual(o, attn_w_o.astype(BF16), 0, h2)
    out = _ffn(h3, norm_ffn[1], w_gate, w_up, w_down, 1, norm_final, final_norm=True)
    return out.reshape(bsz, seq, dm)
```
